```python
import math
import jax, jax.numpy as jnp
from jax import lax
import numpy as np

D_MODEL = 1024
BATCH = 8
SEQ = 2048
DEPTH = 2

DIL_PAIRS = ((128, 1), (512, 4), (2048, 16))
N_DIL_GROUPS = 3
A_SLOTS = 4
A_HEAD_DIM = 64
A_HEADS = N_DIL_GROUPS * A_SLOTS
A_QKV = A_HEADS * A_HEAD_DIM
A_OUT = A_SLOTS * A_HEAD_DIM

B_HEADS = 4
B_HEAD_DIM = 64
B_QK = B_HEADS * 2 * B_HEAD_DIM
B_V = B_HEADS * 2 * B_HEAD_DIM

C_HEADS = 6
C_NOPE = 64
C_ROPE = 32
C_V = 64
C_Q_LORA = 256
C_KV_LORA = 128
C_OUT = C_HEADS * C_V
ROPE_THETA = 10000.0

N_BRANCH = 3
IN_SPLITS = (A_QKV, A_QKV, A_QKV, B_QK, B_QK, B_V, C_Q_LORA, C_KV_LORA, C_ROPE, N_BRANCH * D_MODEL)
IN_COLS = sum(IN_SPLITS)

N_EXPERTS = 16
N_EXPERT_GROUPS = 4
EXPERTS_PER_GROUP = N_EXPERTS // N_EXPERT_GROUPS
TOP_K = 2
D_EXPERT = 512

Q_BLOCK = 128
RMS_EPS = 1e-6
NEG_INF = -1e30

kernel_name = "hybrid_gated_dilated_diff_mla_moe"


def rmsnorm(t, g):
    tf = t.astype(jnp.float32)
    y = tf * lax.rsqrt(jnp.mean(tf * tf, axis=-1, keepdims=True) + RMS_EPS)
    return (y * g.astype(jnp.float32)).astype(t.dtype)


def alibi_slopes(n):
    return 2.0 ** (-8.0 * jnp.arange(1, n + 1, dtype=jnp.float32) / n)


def rope(t, pos):
    d = t.shape[-1]
    half = d // 2
    inv = ROPE_THETA ** (-jnp.arange(half, dtype=jnp.float32) * 2.0 / d)
    ang = pos.astype(jnp.float32)[:, None] * inv[None, :]
    cos = jnp.cos(ang)[:, None, :]
    sin = jnp.sin(ang)[:, None, :]
    tf = t.astype(jnp.float32)
    t1, t2 = tf[..., :half], tf[..., half:]
    return jnp.concatenate([t1 * cos - t2 * sin, t1 * sin + t2 * cos], axis=-1).astype(t.dtype)


def causal_attention(q, k, v, scale, slopes=None):
    B, S, H, dk = q.shape
    nb = S // Q_BLOCK
    key_pos = jnp.arange(S)
    qb = q.reshape(B, nb, Q_BLOCK, H, dk).swapaxes(0, 1)

    def block(args):
        qi, bi = args
        dist = (bi * Q_BLOCK + jnp.arange(Q_BLOCK))[:, None] - key_pos[None, :]
        s = jnp.einsum('bqhd,bkhd->bhqk', qi, k).astype(jnp.float32) * scale
        if slopes is not None:
            s = s - slopes.astype(jnp.float32)[:, None, None] * dist.astype(jnp.float32)
        s = jnp.where(dist >= 0, s, NEG_INF)
        p = jax.nn.softmax(s, axis=-1)
        return jnp.einsum('bhqk,bkhd->bqhd', p.astype(v.dtype), v)

    out = lax.map(block, (qb, jnp.arange(nb)))
    return out.swapaxes(0, 1).reshape(B, S, H, v.shape[-1])


def dilated_group_attention(q, k, v, dilation, steps, slopes):
    B, S, H, d = q.shape
    L = S // dilation
    nb = -(-L // steps)
    Lp = nb * steps

    def to_residue_blocks(t):
        t = t.reshape(B, L, dilation, H, d).transpose(0, 2, 1, 3, 4)
        t = jnp.pad(t, ((0, 0), (0, 0), (0, Lp - L), (0, 0), (0, 0)))
        return t.reshape(B, dilation, nb, steps, H, d)

    def with_prev(t):
        prev = jnp.pad(t[:, :, :-1], ((0, 0), (0, 0), (1, 0), (0, 0), (0, 0), (0, 0)))
        return jnp.concatenate([prev, t], axis=3)

    qb = to_residue_blocks(q)
    kc = with_prev(to_residue_blocks(k))
    vc = with_prev(to_residue_blocks(v))

    s = jnp.einsum('brnqhd,brnkhd->brnhqk', qb, kc).astype(jnp.float32) * (d ** -0.5)
    qi = jnp.arange(nb)[:, None] * steps + jnp.arange(steps)[None, :]
    ki = jnp.arange(nb)[:, None] * steps - steps + jnp.arange(2 * steps)[None, :]
    dist = qi[:, :, None] - ki[:, None, :]
    valid = (dist >= 0) & (dist <= steps) & (ki[:, None, :] >= 0)
    bias = -slopes.astype(jnp.float32)[None, :, None, None] * (dist * dilation).astype(jnp.float32)[:, None]
    s = jnp.where(valid[:, None], s + bias, NEG_INF)
    lse = jax.nn.logsumexp(s, axis=-1)
    p = jnp.exp(s - lse[..., None])
    o = jnp.einsum('brnhqk,brnkhd->brnqhd', p.astype(v.dtype), vc)

    o = o.reshape(B, dilation, Lp, H, d)[:, :, :L].transpose(0, 2, 1, 3, 4).reshape(B, S, H, d)
    lse = lse.transpose(0, 1, 2, 4, 3).reshape(B, dilation, Lp, H)[:, :, :L]
    lse = lse.transpose(0, 2, 1, 3).reshape(B, S, H)
    return o, lse


def moe_ffn(h, w_router, router_bias, w_gate, w_up, w_down):
    def per_row(hr):
        scores = jax.nn.sigmoid((hr @ w_router).astype(jnp.float32))
        biased = scores + router_bias.astype(jnp.float32)
        grouped = biased.reshape(-1, N_EXPERT_GROUPS, EXPERTS_PER_GROUP)
        group_score = lax.top_k(grouped, TOP_K)[0].sum(-1)
        group = jnp.argmax(group_score, axis=-1)
        in_group = jnp.take_along_axis(grouped, group[:, None, None], axis=1)[:, 0]
        _, local = lax.top_k(in_group, TOP_K)
        expert = group[:, None] * EXPERTS_PER_GROUP + local
        w = jnp.take_along_axis(scores, expert, axis=1)
        w = w / jnp.sum(w, axis=-1, keepdims=True)
        combine = jnp.einsum('sk,ske->se', w, jax.nn.one_hot(expert, N_EXPERTS, dtype=jnp.float32))
        a = jnp.einsum('sd,edf->sef', hr, w_gate)
        b = jnp.einsum('sd,edf->sef', hr, w_up)
        act = jax.nn.silu(a) * b * combine[:, :, None].astype(hr.dtype)
        return jnp.einsum('sef,efd->sd', act, w_down)
    return lax.map(per_row, h)


def setup_inputs(seed: int = 0) -> dict:
    key = jax.random.key(seed)
    ks = jax.random.split(key, 24)
    f32 = jnp.float32

    def nrm(k, shape, fan_in, gain=1.0):
        return jax.random.normal(k, shape, f32) * (gain * fan_in ** -0.5)

    def gain_vec(k, shape):
        return 1.0 + 0.02 * jax.random.normal(k, shape, f32)

    res_gain = (2.0 * DEPTH) ** -0.5
    return {
        'x': jax.random.normal(ks[0], (BATCH, SEQ, D_MODEL), f32),
        'w_in': nrm(ks[1], (DEPTH, D_MODEL, IN_COLS), D_MODEL),
        'g_mix': gain_vec(ks[2], (DEPTH, D_MODEL)),
        'p_a': nrm(ks[3], (DEPTH, A_OUT, D_MODEL), A_OUT),
        'p_b': nrm(ks[4], (DEPTH, B_V, D_MODEL), B_V),
        'p_c': nrm(ks[5], (DEPTH, C_OUT, D_MODEL), C_OUT),
        'w_o': nrm(ks[6], (DEPTH, D_MODEL, D_MODEL), D_MODEL, res_gain),
        'g_q': gain_vec(ks[7], (DEPTH, C_Q_LORA)),
        'w_uq': nrm(ks[8], (DEPTH, C_Q_LORA, C_HEADS * (C_NOPE + C_ROPE)), C_Q_LORA),
        'g_kv': gain_vec(ks[9], (DEPTH, C_KV_LORA)),
        'w_ukv': nrm(ks[10], (DEPTH, C_KV_LORA, C_HEADS * (C_NOPE + C_V)), C_KV_LORA),
        'lam_q1': 0.1 * jax.random.normal(ks[11], (DEPTH, B_HEAD_DIM), f32),
        'lam_k1': 0.1 * jax.random.normal(ks[12], (DEPTH, B_HEAD_DIM), f32),
        'lam_q2': 0.1 * jax.random.normal(ks[13], (DEPTH, B_HEAD_DIM), f32),
        'lam_k2': 0.1 * jax.random.normal(ks[14], (DEPTH, B_HEAD_DIM), f32),
        'g_sub': gain_vec(ks[15], (DEPTH, 2 * B_HEAD_DIM)),
        'g_ffn': gain_vec(ks[16], (DEPTH, D_MODEL)),
        'w_router': nrm(ks[17], (D_MODEL, N_EXPERTS), D_MODEL),
        'router_bias': 0.01 * jax.random.normal(ks[18], (N_EXPERTS,), f32),
        'w_gate': nrm(ks[19], (DEPTH, N_EXPERTS, D_MODEL, D_EXPERT), D_MODEL),
        'w_up': nrm(ks[20], (DEPTH, N_EXPERTS, D_MODEL, D_EXPERT), D_MODEL),
        'w_down': nrm(ks[21], (DEPTH, N_EXPERTS, D_EXPERT, D_MODEL), D_EXPERT, res_gain),
        'g_final': gain_vec(ks[22], (D_MODEL,)),
    }


def reference(x, w_in, g_mix, p_a, p_b, p_c, w_o, g_q, w_uq, g_kv, w_ukv,
              lam_q1, lam_k1, lam_q2, lam_k2, g_sub, g_ffn, w_router, router_bias,
              w_gate, w_up, w_down, g_final):
    B, S, _ = x.shape
    pos = jnp.arange(S)
    slopes_a = alibi_slopes(A_HEADS).reshape(N_DIL_GROUPS, A_SLOTS)
    slopes_b = jnp.repeat(alibi_slopes(B_HEADS), 2)
    split_at = np.cumsum(IN_SPLITS)[:-1].tolist()

    for l in range(DEPTH):
        u = rmsnorm(x, g_mix[l])
        qa, ka, va, qd, kd, vd, cq, ckv, kr, gate_pre = jnp.split(u @ w_in[l], split_at, axis=-1)

        shp_a = (B, S, N_DIL_GROUPS, A_SLOTS, A_HEAD_DIM)
        qa, ka, va = qa.reshape(shp_a), ka.reshape(shp_a), va.reshape(shp_a)
        outs, lses = [], []
        for g, (window, dilation) in enumerate(DIL_PAIRS):
            o_g, lse_g = dilated_group_attention(qa[:, :, g], ka[:, :, g], va[:, :, g],
                                                 dilation, window // dilation, slopes_a[g])
            outs.append(o_g)
            lses.append(lse_g)
        alpha = jax.nn.softmax(jnp.stack(lses), axis=0)
        o_a = jnp.sum(alpha[..., None].astype(x.dtype) * jnp.stack(outs), axis=0).reshape(B, S, A_OUT)

        lam_init = 0.8 - 0.6 * math.exp(-0.3 * l)
        lam = (jnp.exp(jnp.sum(lam_q1[l].astype(jnp.float32) * lam_k1[l].astype(jnp.float32)))
               - jnp.exp(jnp.sum(lam_q2[l].astype(jnp.float32) * lam_k2[l].astype(jnp.float32)))
               + lam_init)
        q_d = qd.reshape(B, S, 2 * B_HEADS, B_HEAD_DIM)
        k_d = kd.reshape(B, S, 2 * B_HEADS, B_HEAD_DIM)
        v_d = jnp.repeat(vd.reshape(B, S, B_HEADS, 2 * B_HEAD_DIM), 2, axis=2)
        o_d = causal_attention(q_d, k_d, v_d, B_HEAD_DIM ** -0.5, slopes_b)
        o_d = o_d.reshape(B, S, B_HEADS, 2, 2 * B_HEAD_DIM)
        o_d = o_d[:, :, :, 0] - lam.astype(o_d.dtype) * o_d[:, :, :, 1]
        o_b = (rmsnorm(o_d, g_sub[l]) * (1.0 - lam_init)).reshape(B, S, B_V)

        q_c = (rmsnorm(cq, g_q[l]) @ w_uq[l]).reshape(B, S, C_HEADS, C_NOPE + C_ROPE)
        q_c = jnp.concatenate([q_c[..., :C_NOPE], rope(q_c[..., C_NOPE:], pos)], axis=-1)
        kv = (rmsnorm(ckv, g_kv[l]) @ w_ukv[l]).reshape(B, S, C_HEADS, C_NOPE + C_V)
        k_r = jnp.broadcast_to(rope(kr[:, :, None, :], pos), (B, S, C_HEADS, C_ROPE))
        k_c = jnp.concatenate([kv[..., :C_NOPE], k_r], axis=-1)
        o_c = causal_attention(q_c, k_c, kv[..., C_NOPE:], (C_NOPE + C_ROPE) ** -0.5).reshape(B, S, C_OUT)

        gates = jax.nn.sigmoid(gate_pre).reshape(B, S, N_BRANCH, D_MODEL)
        merged = (gates[:, :, 0] * (o_a @ p_a[l])
                  + gates[:, :, 1] * (o_b @ p_b[l])
                  + gates[:, :, 2] * (o_c @ p_c[l]))
        x = x + merged @ w_o[l]

        x = x + moe_ffn(rmsnorm(x, g_ffn[l]), w_router, router_bias, w_gate[l], w_up[l], w_down[l])

    return rmsnorm(x, g_final)
```

```python
import functools
import math

import jax
import jax.numpy as jnp
import numpy as np
from jax import lax
from jax.experimental import pallas as pl
from jax.experimental.pallas import tpu as pltpu

F32 = jnp.float32
BF16 = jnp.bfloat16

D_MODEL = 1024
DEPTH = 2
DIL_PAIRS = ((128, 1), (512, 4), (2048, 16))
A_SLOTS = 4
A_HEAD_DIM = 64
A_GROUP = A_SLOTS * A_HEAD_DIM
A_QKV = 3 * A_GROUP
A_HEADS = 12
A_STEPS = 128
B_HEADS = 4
B_HEAD_DIM = 64
B_QK = 512
B_V = 512
C_HEADS = 6
C_NOPE = 64
C_ROPE = 32
C_V = 64
C_Q_LORA = 256
C_KV_LORA = 128
C_PAD = 128
ROPE_THETA = 10000.0
N_EXPERTS = 16
N_EXPERT_GROUPS = 4
EXPERTS_PER_GROUP = 4
D_EXPERT = 512
RMS_EPS = 1e-6
NEG_INF = -1e30

LANES = 128
VMEM_LIMIT = 56 * 1024 * 1024

COL_A = 0
COL_B = COL_A + 3 * A_QKV
COL_CQ = COL_B + 2 * B_QK + B_V
COL_CKV = COL_CQ + C_Q_LORA
COL_KRM = COL_CKV + C_KV_LORA
COL_KRS = COL_KRM + LANES
COL_GATE = COL_KRS + LANES
COL_END = COL_GATE + 3 * D_MODEL


def _rms(x, g):
    return x * lax.rsqrt(jnp.mean(x * x, axis=-1, keepdims=True) + RMS_EPS) * g


def _sigmoid(z):
    return 1.0 / (1.0 + jnp.exp(-z))


def _dot(a, b):
    return jnp.dot(a, b, preferred_element_type=F32)


def _dot_nt(a, b):
    return lax.dot_general(a, b, (((1,), (1,)), ((), ())), preferred_element_type=F32)


def _params(sem):
    return pltpu.CompilerParams(dimension_semantics=sem, vmem_limit_bytes=VMEM_LIMIT)


def _const_spec(shape):
    nd = len(shape)
    return pl.BlockSpec(shape, lambda *_: (0,) * nd, pipeline_mode=pl.Buffered(1))


def _inproj_kernel(x_ref, gmix_ref, w_ref, gq_ref, gkv_ref, wq2_ref, wkv2_ref, ta_ref, tb_ref,
                   oa_ref, ob_ref, qc_ref, kc_ref, vc_ref, og_ref):
    u = _rms(x_ref[...], gmix_ref[...]).astype(BF16)

    def mm(c0, c1):
        return _dot(u, w_ref[:, c0:c1])

    a_scale = A_HEAD_DIM ** -0.5
    oa_ref[:, 0:A_QKV] = (mm(COL_A, COL_A + A_QKV) * a_scale).astype(BF16)
    oa_ref[:, A_QKV:2 * A_QKV] = mm(COL_A + A_QKV, COL_A + 2 * A_QKV).astype(BF16)
    oa_ref[:, 2 * A_QKV:3 * A_QKV] = mm(COL_A + 2 * A_QKV, COL_B).astype(BF16)

    b_scale = B_HEAD_DIM ** -0.5
    ob_ref[:, 0:B_QK] = (mm(COL_B, COL_B + B_QK) * b_scale).astype(BF16)
    ob_ref[:, B_QK:2 * B_QK] = mm(COL_B + B_QK, COL_B + 2 * B_QK).astype(BF16)
    ob_ref[:, 2 * B_QK:2 * B_QK + B_V] = mm(COL_B + 2 * B_QK, COL_CQ).astype(BF16)

    lat = mm(COL_CQ, COL_GATE)
    cqn = _rms(lat[:, 0:C_Q_LORA], gq_ref[...]).astype(BF16)
    ckvn = _rms(lat[:, C_Q_LORA:C_Q_LORA + C_KV_LORA], gkv_ref[...]).astype(BF16)
    krm = lat[:, COL_KRM - COL_CQ:COL_KRS - COL_CQ]
    krs = lat[:, COL_KRS - COL_CQ:COL_GATE - COL_CQ]
    q2 = _dot(cqn, wq2_ref[...])
    kv2 = _dot(ckvn, wkv2_ref[...])
    ta = ta_ref[...]
    tb = tb_ref[...]
    krot = krm * ta + krs * tb
    c_scale = (C_NOPE + C_ROPE) ** -0.5
    nq = C_HEADS * C_PAD
    for h in range(C_HEADS):
        sl = slice(h * C_PAD, (h + 1) * C_PAD)
        qm = q2[:, h * C_PAD:(h + 1) * C_PAD]
        qs = q2[:, nq + h * C_PAD:nq + (h + 1) * C_PAD]
        qc_ref[:, sl] = ((qm * ta + qs * tb) * c_scale).astype(BF16)
        kc_ref[:, sl] = (kv2[:, h * C_PAD:(h + 1) * C_PAD] + krot).astype(BF16)
    vc_ref[...] = kv2[:, nq:nq + C_HEADS * C_V].astype(BF16)

    gchunk = 768
    for j in range(3 * D_MODEL // gchunk):
        z = mm(COL_GATE + j * gchunk, COL_GATE + (j + 1) * gchunk)
        og_ref[:, j * gchunk:(j + 1) * gchunk] = _sigmoid(z).astype(BF16)


def _inproj(xf, gmix, w_pack, gq, gkv, wq2, wkv2, ta, tb, seq):
    T = xf.shape[0]
    tm = 512
    n_pos_blocks = seq // tm
    row = lambda w: pl.BlockSpec((tm, w), lambda i: (i, 0))
    tab = pl.BlockSpec((tm, LANES), lambda i: (i % n_pos_blocks, 0))
    widths = (3 * A_QKV, 2 * B_QK + B_V, C_HEADS * C_PAD, C_HEADS * C_PAD, C_HEADS * C_V, 3 * D_MODEL)
    return pl.pallas_call(
        _inproj_kernel,
        grid=(T // tm,),
        in_specs=[row(D_MODEL), _const_spec((1, D_MODEL)), _const_spec(w_pack.shape),
                  _const_spec((1, C_Q_LORA)), _const_spec((1, C_KV_LORA)),
                  _const_spec(wq2.shape), _const_spec(wkv2.shape), tab, tab],
        out_specs=[row(w) for w in widths],
        out_shape=[jax.ShapeDtypeStruct((T, w), BF16) for w in widths],
        compiler_params=_params(("parallel",)),
        name="inproj",
    )(xf, gmix, w_pack, gq, gkv, wq2, wkv2, ta, tb)


def _dilated_kernel(q_ref, k_ref, v_ref, o_ref, l_ref, *, win, sub_len, dilation, slopes):
    n = pl.program_id(2)
    q = q_ref[0]
    start = pl.multiple_of(jnp.clip((n - 1) * A_STEPS, 0, sub_len - win), A_STEPS)
    k = k_ref[0, pl.ds(start, win), :]
    v = v_ref[0, pl.ds(start, win), :]
    qpos = n * A_STEPS + lax.broadcasted_iota(jnp.int32, (A_STEPS, 1), 0)
    kpos = start + lax.broadcasted_iota(jnp.int32, (1, win), 1)
    dist = qpos - kpos
    valid = (dist >= 0) & (dist <= A_STEPS)
    distf = dist.astype(F32)
    head_of_lane = lax.broadcasted_iota(jnp.int32, (1, A_GROUP), 1) // A_HEAD_DIM
    out = jnp.zeros((A_STEPS, A_GROUP), F32)
    lse = jnp.zeros((A_STEPS, A_GROUP), F32)
    for h in range(A_SLOTS):
        own = head_of_lane == h
        s = _dot_nt(jnp.where(own, q, jnp.zeros_like(q)), k)
        s = jnp.where(valid, s - (slopes[h] * dilation) * distf, NEG_INF)
        m = jnp.max(s, axis=-1, keepdims=True)
        p = jnp.exp(s - m)
        l = jnp.sum(p, axis=-1, keepdims=True)
        pv = _dot(p.astype(BF16), v)
        out = jnp.where(own, pv / l, out)
        lse = jnp.where(own, m + jnp.log(l), lse)
    o_ref[0] = out
    l_ref[0] = lse


def _dilated(qkva, group, dilation, slopes, batch, seq):
    sub_len = seq // dilation
    nb = sub_len // A_STEPS
    win = min(2 * A_STEPS, sub_len)
    cols = 3 * A_QKV // A_GROUP
    x = qkva.reshape(batch, sub_len, dilation * 3 * A_QKV)
    kern = functools.partial(_dilated_kernel, win=win, sub_len=sub_len, dilation=dilation, slopes=slopes)
    q_spec = pl.BlockSpec((1, A_STEPS, A_GROUP), lambda b, c, n: (b, n, c * cols + group))
    k_spec = pl.BlockSpec((1, sub_len, A_GROUP), lambda b, c, n: (b, 0, c * cols + 3 + group))
    v_spec = pl.BlockSpec((1, sub_len, A_GROUP), lambda b, c, n: (b, 0, c * cols + 6 + group))
    o_spec = pl.BlockSpec((1, A_STEPS, A_GROUP), lambda b, c, n: (b, n, c))
    shape = jax.ShapeDtypeStruct((batch, sub_len, dilation * A_GROUP), F32)
    o, lse = pl.pallas_call(
        kern,
        grid=(batch, dilation, nb),
        in_specs=[q_spec, k_spec, v_spec],
        out_specs=[o_spec, o_spec],
        out_shape=[shape, shape],
        compiler_params=_params(("parallel", "parallel", "arbitrary")),
        name=f"dilated{group}",
    )(x, x, x)
    return o.reshape(batch * seq, A_GROUP), lse.reshape(batch * seq, A_GROUP)


def _flash_update(m_ref, l_ref, acc_ref, s, v):
    m_prev = m_ref[...]
    m_new = jnp.maximum(m_prev, jnp.max(s, axis=-1, keepdims=True))
    alpha = jnp.exp(m_prev - m_new)
    p = jnp.exp(s - m_new)
    l_ref[...] = alpha * l_ref[...] + jnp.sum(p, axis=-1, keepdims=True)
    acc_ref[...] = alpha * acc_ref[...] + _dot(p.astype(BF16), v)
    m_ref[...] = m_new


def _causal_mask(t):
    return lax.broadcasted_iota(jnp.int32, (t, t), 0) >= lax.broadcasted_iota(jnp.int32, (t, t), 1)


def _diff_kernel(lam_ref, gsub_ref, q_ref, k_ref, v_ref, o_ref,
                 m1, l1, acc1, m2, l2, acc2, *, t, lam_init):
    h = pl.program_id(1)
    qi = pl.program_id(2)
    q = q_ref[0]
    lane = lax.broadcasted_iota(jnp.int32, (1, LANES), 1)
    zero = jnp.zeros_like(q)
    q1 = jnp.where(lane < B_HEAD_DIM, q, zero)
    q2 = jnp.where(lane >= B_HEAD_DIM, q, zero)
    slope = jnp.exp2(jnp.full((1, 1), -2.0, F32) * (h + 1).astype(F32))
    for m_ref, l_ref, acc_ref in ((m1, l1, acc1), (m2, l2, acc2)):
        m_ref[...] = jnp.full(m_ref.shape, NEG_INF, F32)
        l_ref[...] = jnp.zeros(l_ref.shape, F32)
        acc_ref[...] = jnp.zeros(acc_ref.shape, F32)
    col = lax.broadcasted_iota(jnp.int32, (1, t), 1)

    def step(j, masked):
        k = k_ref[0, pl.ds(pl.multiple_of(j * t, t), t), :]
        v = v_ref[0, pl.ds(pl.multiple_of(j * t, t), t), :]
        bias = slope * ((j - qi) * t + col).astype(F32)
        s1 = _dot_nt(q1, k) + bias
        s2 = _dot_nt(q2, k) + bias
        if masked:
            keep = _causal_mask(t)
            s1 = jnp.where(keep, s1, NEG_INF)
            s2 = jnp.where(keep, s2, NEG_INF)
        _flash_update(m1, l1, acc1, s1, v)
        _flash_update(m2, l2, acc2, s2, v)

    def body(j, carry):
        step(j, False)
        return carry

    lax.fori_loop(0, qi, body, 0)
    step(qi, True)

    lam_rows = lam_ref[...]
    lam = (jnp.exp(jnp.sum(lam_rows[0:1] * lam_rows[1:2], axis=-1, keepdims=True))
           - jnp.exp(jnp.sum(lam_rows[2:3] * lam_rows[3:4], axis=-1, keepdims=True)) + lam_init)
    od = acc1[...] / l1[...] - lam * (acc2[...] / l2[...])
    o_ref[0] = (_rms(od, gsub_ref[...]) * (1.0 - lam_init)).astype(BF16)


def _diff(qkvb, lam_rows, gsub, lam_init, batch, seq):
    t = 512
    x = qkvb.reshape(batch, seq, 2 * B_QK + B_V)
    kern = functools.partial(_diff_kernel, t=t, lam_init=lam_init)
    vec = lambda: pltpu.VMEM((t, 1), F32)
    acc = lambda: pltpu.VMEM((t, LANES), F32)
    out = pl.pallas_call(
        kern,
        grid=(batch, B_HEADS, seq // t),
        in_specs=[_const_spec(lam_rows.shape), _const_spec(gsub.shape),
                  pl.BlockSpec((1, t, LANES), lambda b, h, i: (b, i, h)),
                  pl.BlockSpec((1, seq, LANES), lambda b, h, i: (b, 0, B_HEADS + h)),
                  pl.BlockSpec((1, seq, LANES), lambda b, h, i: (b, 0, 2 * B_HEADS + h))],
        out_specs=pl.BlockSpec((1, t, LANES), lambda b, h, i: (b, i, h)),
        out_shape=jax.ShapeDtypeStruct((batch, seq, B_V), BF16),
        scratch_shapes=[vec(), vec(), acc(), vec(), vec(), acc()],
        compiler_params=_params(("parallel", "parallel", "arbitrary")),
        name="diff_attn",
    )(lam_rows, gsub, x, x, x)
    return out.reshape(batch * seq, B_V)


def _latent_kernel(q_ref, k_ref, v_ref, o_ref, m0, l0, acc0, m1, l1, acc1, *, t):
    qi = pl.program_id(2)
    q = q_ref[0]
    state = ((m0, l0, acc0), (m1, l1, acc1))
    for m_ref, l_ref, acc_ref in state:
        m_ref[...] = jnp.full(m_ref.shape, NEG_INF, F32)
        l_ref[...] = jnp.zeros(l_ref.shape, F32)
        acc_ref[...] = jnp.zeros(acc_ref.shape, F32)

    def step(j, masked):
        rows = pl.ds(pl.multiple_of(j * t, t), t)
        k = k_ref[0, rows, :]
        v = v_ref[0, rows, :]
        for hh, (m_ref, l_ref, acc_ref) in enumerate(state):
            s = _dot_nt(q[:, hh * C_PAD:(hh + 1) * C_PAD], k[:, hh * C_PAD:(hh + 1) * C_PAD])
            if masked:
                s = jnp.where(_causal_mask(t), s, NEG_INF)
            _flash_update(m_ref, l_ref, acc_ref, s, v)

    def body(j, carry):
        step(j, False)
        return carry

    lax.fori_loop(0, qi, body, 0)
    step(qi, True)
    lane = lax.broadcasted_iota(jnp.int32, (1, 2 * C_V), 1)
    o_ref[0] = jnp.where(lane < C_V, acc0[...] / l0[...], acc1[...] / l1[...]).astype(BF16)


def _latent(qc, kc, vc, batch, seq):
    t = 512
    pairs = C_HEADS // 2
    kern = functools.partial(_latent_kernel, t=t)
    vec = lambda: pltpu.VMEM((t, 1), F32)
    acc = lambda: pltpu.VMEM((t, 2 * C_V), F32)
    out = pl.pallas_call(
        kern,
        grid=(batch, pairs, seq // t),
        in_specs=[pl.BlockSpec((1, t, 2 * C_PAD), lambda b, h, i: (b, i, h)),
                  pl.BlockSpec((1, seq, 2 * C_PAD), lambda b, h, i: (b, 0, h)),
                  pl.BlockSpec((1, seq, 2 * C_V), lambda b, h, i: (b, 0, h))],
        out_specs=pl.BlockSpec((1, t, 2 * C_V), lambda b, h, i: (b, i, h)),
        out_shape=jax.ShapeDtypeStruct((batch, seq, C_HEADS * C_V), BF16),
        scratch_shapes=[vec(), vec(), acc(), vec(), vec(), acc()],
        compiler_params=_params(("parallel", "parallel", "arbitrary")),
        name="latent_attn",
    )(qc.reshape(batch, seq, -1), kc.reshape(batch, seq, -1), vc.reshape(batch, seq, -1))
    return out.reshape(batch * seq, C_HEADS * C_V)


def _route_columns(sc, bs):
    gscore = []
    for g in range(N_EXPERT_GROUPS):
        a, b, c, d = bs[4 * g:4 * g + 4]
        hi1, lo1 = jnp.maximum(a, b), jnp.minimum(a, b)
        hi2, lo2 = jnp.maximum(c, d), jnp.minimum(c, d)
        gscore.append(jnp.maximum(hi1, hi2) + jnp.maximum(jnp.minimum(hi1, hi2), jnp.maximum(lo1, lo2)))
    picked = []
    for g in range(N_EXPERT_GROUPS):
        ok = None
        for i in range(N_EXPERT_GROUPS):
            if i == g:
                continue
            c = gscore[g] > gscore[i] if i < g else gscore[g] >= gscore[i]
            ok = c if ok is None else ok & c
        picked.append(ok)
    sel = []
    for g in range(N_EXPERT_GROUPS):
        for j in range(EXPERTS_PER_GROUP):
            vj = bs[4 * g + j]
            rank = jnp.zeros(vj.shape, jnp.int32)
            for i in range(EXPERTS_PER_GROUP):
                if i == j:
                    continue
                vi = bs[4 * g + i]
                ahead = vi >= vj if i < j else vi > vj
                rank = rank + ahead.astype(jnp.int32)
            sel.append(picked[g] & (rank < 2))
    picked_w = [jnp.where(sel[e], sc[e], 0.0) for e in range(N_EXPERTS)]
    total = picked_w[0]
    for e in range(1, N_EXPERTS):
        total = total + picked_w[e]
    return [w / total for w in picked_w]


def _outproj_kernel(oa0, la0, oa1, la1, oa2, la2, ob_ref, oc_ref, gt_ref, x_ref,
                    pa_ref, pb_ref, pc_ref, wo_ref, gffn_ref, wrh_ref, wrl_ref, rb_ref,
                    xn_ref, h_ref, comb_ref):
    l0, l1, l2 = la0[...], la1[...], la2[...]
    mx = jnp.maximum(jnp.maximum(l0, l1), l2)
    e0, e1, e2 = jnp.exp(l0 - mx), jnp.exp(l1 - mx), jnp.exp(l2 - mx)
    o_a = (e0 * oa0[...] + e1 * oa1[...] + e2 * oa2[...]) / (e0 + e1 + e2)
    ya = _dot(o_a.astype(BF16), pa_ref[...])
    yb = _dot(ob_ref[...], pb_ref[...])
    yc = _dot(oc_ref[...], pc_ref[...])
    merged = (gt_ref[:, 0:D_MODEL].astype(F32) * ya
              + gt_ref[:, D_MODEL:2 * D_MODEL].astype(F32) * yb
              + gt_ref[:, 2 * D_MODEL:3 * D_MODEL].astype(F32) * yc)
    xn = x_ref[...] + _dot(merged.astype(BF16), wo_ref[...])
    xn_ref[...] = xn
    hn = _rms(xn, gffn_ref[...])
    h_ref[...] = hn.astype(BF16)

    h_hi = hn.astype(BF16)
    h_lo = (hn - h_hi.astype(F32)).astype(BF16)
    logits = _dot(h_hi, wrh_ref[...]) + _dot(h_lo, wrh_ref[...]) + _dot(h_hi, wrl_ref[...])
    scores = _sigmoid(logits)
    biased = scores + rb_ref[...]
    sc = [scores[:, e:e + 1] for e in range(N_EXPERTS)]
    bs = [biased[:, e:e + 1] for e in range(N_EXPERTS)]
    comb_cols = _route_columns(sc, bs)
    lane = lax.broadcasted_iota(jnp.int32, (1, LANES), 1)
    comb = jnp.zeros(scores.shape, F32)
    for e in range(N_EXPERTS):
        comb = jnp.where(lane == e, comb_cols[e], comb)
    comb_ref[...] = comb


def _outproj(oas, las, ob, oc, gates, xf, pa, pb, pc, wo, gffn, wr_hi, wr_lo, rbias):
    T = xf.shape[0]
    tm = 256
    row = lambda w: pl.BlockSpec((tm, w), lambda i: (i, 0))
    ins, specs = [], []
    for o, l in zip(oas, las):
        ins += [o, l]
        specs += [row(A_GROUP), row(A_GROUP)]
    ins += [ob, oc, gates, xf, pa, pb, pc, wo, gffn, wr_hi, wr_lo, rbias]
    specs += [row(B_V), row(C_HEADS * C_V), row(3 * D_MODEL), row(D_MODEL)]
    specs += [_const_spec(a.shape) for a in (pa, pb, pc, wo, gffn, wr_hi, wr_lo, rbias)]
    return pl.pallas_call(
        _outproj_kernel,
        grid=(T // tm,),
        in_specs=specs,
        out_specs=[row(D_MODEL), row(D_MODEL), row(LANES)],
        out_shape=[jax.ShapeDtypeStruct((T, D_MODEL), F32), jax.ShapeDtypeStruct((T, D_MODEL), BF16),
                   jax.ShapeDtypeStruct((T, LANES), F32)],
        compiler_params=_params(("parallel",)),
        name="outproj",
    )(*ins)


def _experts_kernel(h_ref, comb_ref, x_ref, wg_ref, wu_ref, wd_ref, gfin_ref, o_ref, acc_ref, *, final):
    e = pl.program_id(1)

    @pl.when(e == 0)
    def _():
        acc_ref[...] = jnp.zeros(acc_ref.shape, F32)

    h = h_ref[...]
    a = _dot(h, wg_ref[0])
    b = _dot(h, wu_ref[0])
    lane = lax.broadcasted_iota(jnp.int32, (1, LANES), 1)
    c = jnp.sum(jnp.where(lane == e, comb_ref[...], 0.0), axis=-1, keepdims=True)
    act = a * _sigmoid(a) * b * c
    acc_ref[...] += _dot(act.astype(BF16), wd_ref[0])

    @pl.when(e == N_EXPERTS - 1)
    def _():
        y = x_ref[...] + acc_ref[...]
        if final:
            y = _rms(y, gfin_ref[...])
        o_ref[...] = y


def _experts(h, comb, xf, wg, wu, wd, gfin, final):
    T = xf.shape[0]
    tm = 1024
    kern = functools.partial(_experts_kernel, final=final)
    row = lambda w: pl.BlockSpec((tm, w), lambda i, e: (i, 0))
    return pl.pallas_call(
        kern,
        grid=(T // tm, N_EXPERTS),
        in_specs=[row(D_MODEL), row(LANES), row(D_MODEL),
                  pl.BlockSpec((1, D_MODEL, D_EXPERT), lambda i, e: (e, 0, 0)),
                  pl.BlockSpec((1, D_MODEL, D_EXPERT), lambda i, e: (e, 0, 0)),
                  pl.BlockSpec((1, D_EXPERT, D_MODEL), lambda i, e: (e, 0, 0)),
                  pl.BlockSpec((1, D_MODEL), lambda i, e: (0, 0))],
        out_specs=row(D_MODEL),
        out_shape=jax.ShapeDtypeStruct((T, D_MODEL), F32),
        scratch_shapes=[pltpu.VMEM((tm, D_MODEL), F32)],
        compiler_params=_params(("parallel", "arbitrary")),
        name="experts",
    )(h, comb, xf, wg, wu, wd, gfin)


def _alibi_slopes(n):
    return [2.0 ** (-8.0 * h / n) for h in range(1, n + 1)]


def _rope_tables(seq):
    half = C_ROPE // 2
    inv = ROPE_THETA ** (-jnp.arange(half, dtype=F32) * 2.0 / C_ROPE)
    ang = jnp.arange(seq, dtype=F32)[:, None] * inv[None, :]
    cos, sin = jnp.cos(ang), jnp.sin(ang)
    ones = jnp.ones((seq, C_NOPE), F32)
    zeros_n = jnp.zeros((seq, C_NOPE), F32)
    tail = jnp.zeros((seq, C_PAD - C_NOPE - C_ROPE), F32)
    ta = jnp.concatenate([ones, cos, cos, tail], axis=1)
    tb = jnp.concatenate([zeros_n, -sin, sin, tail], axis=1)
    return ta, tb


def _pack_in_weight(w):
    a_b_lat = w[:, 0:4224]
    kr = w[:, 4224:4256]
    gate = w[:, 4256:]
    half = C_ROPE // 2
    z = lambda n: jnp.zeros((w.shape[0], n), w.dtype)
    kr_main = jnp.concatenate([z(C_NOPE), kr, z(C_PAD - C_NOPE - C_ROPE)], axis=1)
    kr_swap = jnp.concatenate([z(C_NOPE), kr[:, half:], kr[:, :half], z(C_PAD - C_NOPE - C_ROPE)], axis=1)
    return jnp.concatenate([a_b_lat, kr_main, kr_swap, gate], axis=1).astype(BF16)


def _pack_latent_weights(w_uq, w_ukv):
    dq = C_NOPE + C_ROPE
    half = C_ROPE // 2
    rows_q = w_uq.shape[0]
    zq = lambda n: jnp.zeros((rows_q, n), w_uq.dtype)
    main, swap = [], []
    for h in range(C_HEADS):
        blk = w_uq[:, h * dq:(h + 1) * dq]
        main += [blk, zq(C_PAD - dq)]
        swap += [zq(C_NOPE), blk[:, C_NOPE + half:], blk[:, C_NOPE:C_NOPE + half], zq(C_PAD - dq)]
    wq2 = jnp.concatenate(main + swap, axis=1).astype(BF16)
    rows_k = w_ukv.shape[0]
    zk = jnp.zeros((rows_k, C_PAD - C_NOPE), w_ukv.dtype)
    dkv = C_NOPE + C_V
    kparts, vparts = [], []
    for h in range(C_HEADS):
        blk = w_ukv[:, h * dkv:(h + 1) * dkv]
        kparts += [blk[:, :C_NOPE], zk]
        vparts.append(blk[:, C_NOPE:])
    wkv2 = jnp.concatenate(kparts + vparts, axis=1).astype(BF16)
    return wq2, wkv2


def kernel(x, w_in, g_mix, p_a, p_b, p_c, w_o, g_q, w_uq, g_kv, w_ukv, lam_q1, lam_k1, lam_q2, lam_k2,
           g_sub, g_ffn, w_router, router_bias, w_gate, w_up, w_down, g_final):
    batch, seq, _ = x.shape
    xf = x.reshape(batch * seq, D_MODEL)
    ta, tb = _rope_tables(seq)
    slopes_a = _alibi_slopes(A_HEADS)
    wr = jnp.pad(w_router, ((0, 0), (0, LANES - N_EXPERTS)))
    wr_hi = wr.astype(BF16)
    wr_lo = (wr - wr_hi.astype(F32)).astype(BF16)
    rbias = jnp.pad(router_bias, (0, LANES - N_EXPERTS)).reshape(1, LANES)
    gfin = g_final.reshape(1, D_MODEL)

    for l in range(DEPTH):
        w_pack = _pack_in_weight(w_in[l])
        wq2, wkv2 = _pack_latent_weights(w_uq[l], w_ukv[l])
        qkva, qkvb, qc, kc, vc, gates = _inproj(
            xf, g_mix[l].reshape(1, -1), w_pack, g_q[l].reshape(1, -1), g_kv[l].reshape(1, -1),
            wq2, wkv2, ta, tb, seq)

        oas, las = [], []
        for g, (_, dilation) in enumerate(DIL_PAIRS):
            o, lse = _dilated(qkva, g, dilation, slopes_a[g * A_SLOTS:(g + 1) * A_SLOTS], batch, seq)
            oas.append(o)
            las.append(lse)

        lam_init = 0.8 - 0.6 * math.exp(-0.3 * l)
        lam_rows = jnp.stack([lam_q1[l], lam_k1[l], lam_q2[l], lam_k2[l]])
        ob = _diff(qkvb, lam_rows, g_sub[l].reshape(1, -1), lam_init, batch, seq)
        oc = _latent(qc, kc, vc, batch, seq)

        xf, h, comb = _outproj(oas, las, ob, oc, gates, xf,
                               p_a[l].astype(BF16), p_b[l].astype(BF16), p_c[l].astype(BF16),
                               w_o[l].astype(BF16), g_ffn[l].reshape(1, -1), wr_hi, wr_lo, rbias)
        xf = _experts(h, comb, xf, w_gate[l].astype(BF16), w_up[l].astype(BF16), w_down[l].astype(BF16),
                      gfin, final=(l == DEPTH - 1))
    return xf.reshape(batch, seq, D_MODEL)
```

```python
import functools
import math

import jax
import jax.numpy as jnp
import numpy as np
from jax import lax
from jax.experimental import pallas as pl
from jax.experimental.pallas import tpu as pltpu

F32 = jnp.float32
BF16 = jnp.bfloat16

D_MODEL = 1024
DEPTH = 2
DIL_PAIRS = ((128, 1), (512, 4), (2048, 16))
A_SLOTS = 4
A_HEAD_DIM = 64
A_GROUP = A_SLOTS * A_HEAD_DIM
A_QKV = 3 * A_GROUP
A_HEADS = 12
A_STEPS = 128
B_HEADS = 4
B_HEAD_DIM = 64
B_QK = 512
B_V = 512
C_HEADS = 6
C_NOPE = 64
C_ROPE = 32
C_V = 64
C_Q_LORA = 256
C_KV_LORA = 128
C_PAD = 128
ROPE_THETA = 10000.0
N_EXPERTS = 16
N_EXPERT_GROUPS = 4
EXPERTS_PER_GROUP = 4
D_EXPERT = 512
RMS_EPS = 1e-6
NEG_INF = -1e30

LANES = 128
LOG2E = 1.4426950408889634
VMEM_LIMIT = 56 * 1024 * 1024

COL_A = 0
COL_B = COL_A + 3 * A_QKV
COL_CQ = COL_B + 2 * B_QK + B_V
COL_CKV = COL_CQ + C_Q_LORA
COL_KRM = COL_CKV + C_KV_LORA
COL_KRS = COL_KRM + LANES
COL_GATE = COL_KRS + LANES
COL_END = COL_GATE + 3 * D_MODEL


def _rms(x, g):
    return x * lax.rsqrt(jnp.mean(x * x, axis=-1, keepdims=True) + RMS_EPS) * g


def _sigmoid(z):
    return 1.0 / (1.0 + jnp.exp(-z))


def _dot(a, b):
    return jnp.dot(a, b, preferred_element_type=F32)


def _dot_nt(a, b):
    return lax.dot_general(a, b, (((1,), (1,)), ((), ())), preferred_element_type=F32)


def _params(sem):
    return pltpu.CompilerParams(dimension_semantics=sem, vmem_limit_bytes=VMEM_LIMIT)


def _const_spec(shape):
    nd = len(shape)
    return pl.BlockSpec(shape, lambda *_: (0,) * nd, pipeline_mode=pl.Buffered(1))


def _inproj_kernel(x_ref, gmix_ref, w_ref, gq_ref, gkv_ref, wq2_ref, wkv2_ref, ta_ref, tb_ref,
                   oa_ref, ob_ref, qc_ref, kc_ref, vc_ref, og_ref):
    u = _rms(x_ref[...], gmix_ref[...]).astype(BF16)

    def mm(c0, c1):
        return _dot(u, w_ref[:, c0:c1])

    a_scale = A_HEAD_DIM ** -0.5
    oa_ref[:, 0:A_QKV] = (mm(COL_A, COL_A + A_QKV) * a_scale).astype(BF16)
    oa_ref[:, A_QKV:2 * A_QKV] = mm(COL_A + A_QKV, COL_A + 2 * A_QKV).astype(BF16)
    oa_ref[:, 2 * A_QKV:3 * A_QKV] = mm(COL_A + 2 * A_QKV, COL_B).astype(BF16)

    b_scale = B_HEAD_DIM ** -0.5 * LOG2E
    ob_ref[:, 0:B_QK] = (mm(COL_B, COL_B + B_QK) * b_scale).astype(BF16)
    ob_ref[:, B_QK:2 * B_QK] = mm(COL_B + B_QK, COL_B + 2 * B_QK).astype(BF16)
    ob_ref[:, 2 * B_QK:2 * B_QK + B_V] = mm(COL_B + 2 * B_QK, COL_CQ).astype(BF16)

    lat = mm(COL_CQ, COL_GATE)
    cqn = _rms(lat[:, 0:C_Q_LORA], gq_ref[...]).astype(BF16)
    ckvn = _rms(lat[:, C_Q_LORA:C_Q_LORA + C_KV_LORA], gkv_ref[...]).astype(BF16)
    krm = lat[:, COL_KRM - COL_CQ:COL_KRS - COL_CQ]
    krs = lat[:, COL_KRS - COL_CQ:COL_GATE - COL_CQ]
    q2 = _dot(cqn, wq2_ref[...])
    kv2 = _dot(ckvn, wkv2_ref[...])
    ta = ta_ref[...]
    tb = tb_ref[...]
    krot = krm * ta + krs * tb
    c_scale = (C_NOPE + C_ROPE) ** -0.5 * LOG2E
    nq = C_HEADS * C_PAD
    for h in range(C_HEADS):
        sl = slice(h * C_PAD, (h + 1) * C_PAD)
        qm = q2[:, h * C_PAD:(h + 1) * C_PAD]
        qs = q2[:, nq + h * C_PAD:nq + (h + 1) * C_PAD]
        qc_ref[:, sl] = ((qm * ta + qs * tb) * c_scale).astype(BF16)
        kc_ref[:, sl] = (kv2[:, h * C_PAD:(h + 1) * C_PAD] + krot).astype(BF16)
    vc_ref[...] = kv2[:, nq:nq + C_HEADS * C_V].astype(BF16)

    gchunk = 768
    for j in range(3 * D_MODEL // gchunk):
        z = mm(COL_GATE + j * gchunk, COL_GATE + (j + 1) * gchunk)
        og_ref[:, j * gchunk:(j + 1) * gchunk] = _sigmoid(z).astype(BF16)


def _inproj(xf, gmix, w_pack, gq, gkv, wq2, wkv2, ta, tb, seq):
    T = xf.shape[0]
    tm = 512
    n_pos_blocks = seq // tm
    row = lambda w: pl.BlockSpec((tm, w), lambda i: (i, 0))
    tab = pl.BlockSpec((tm, LANES), lambda i: (i % n_pos_blocks, 0))
    widths = (3 * A_QKV, 2 * B_QK + B_V, C_HEADS * C_PAD, C_HEADS * C_PAD, C_HEADS * C_V, 3 * D_MODEL)
    return pl.pallas_call(
        _inproj_kernel,
        grid=(T // tm,),
        in_specs=[row(D_MODEL), _const_spec((1, D_MODEL)), _const_spec(w_pack.shape),
                  _const_spec((1, C_Q_LORA)), _const_spec((1, C_KV_LORA)),
                  _const_spec(wq2.shape), _const_spec(wkv2.shape), tab, tab],
        out_specs=[row(w) for w in widths],
        out_shape=[jax.ShapeDtypeStruct((T, w), BF16) for w in widths],
        compiler_params=_params(("parallel",)),
        name="inproj",
    )(xf, gmix, w_pack, gq, gkv, wq2, wkv2, ta, tb)


def _dilated_kernel(q_ref, k_ref, v_ref, o_ref, l_ref, *, win, sub_len, dilation, slopes):
    n = pl.program_id(2)
    q = q_ref[0]
    start = pl.multiple_of(jnp.clip((n - 1) * A_STEPS, 0, sub_len - win), A_STEPS)
    k = k_ref[0, pl.ds(start, win), :]
    v = v_ref[0, pl.ds(start, win), :]
    qpos = n * A_STEPS + lax.broadcasted_iota(jnp.int32, (A_STEPS, 1), 0)
    kpos = start + lax.broadcasted_iota(jnp.int32, (1, win), 1)
    dist = qpos - kpos
    valid = (dist >= 0) & (dist <= A_STEPS)
    distf = dist.astype(F32)
    head_of_lane = lax.broadcasted_iota(jnp.int32, (1, A_GROUP), 1) // A_HEAD_DIM
    out = jnp.zeros((A_STEPS, A_GROUP), F32)
    lse = jnp.zeros((A_STEPS, A_GROUP), F32)
    for h in range(A_SLOTS):
        own = head_of_lane == h
        s = _dot_nt(jnp.where(own, q, jnp.zeros_like(q)), k)
        s = jnp.where(valid, s - (slopes[h] * dilation) * distf, NEG_INF)
        m = jnp.max(s, axis=-1, keepdims=True)
        p = jnp.exp(s - m)
        l = jnp.sum(p, axis=-1, keepdims=True)
        pv = _dot(p.astype(BF16), v)
        out = jnp.where(own, pv / l, out)
        lse = jnp.where(own, m + jnp.log(l), lse)
    o_ref[0] = out
    l_ref[0] = lse


def _dilated(qkva, group, dilation, slopes, batch, seq):
    sub_len = seq // dilation
    nb = sub_len // A_STEPS
    win = min(2 * A_STEPS, sub_len)
    cols = 3 * A_QKV // A_GROUP
    x = qkva.reshape(batch, sub_len, dilation * 3 * A_QKV)
    kern = functools.partial(_dilated_kernel, win=win, sub_len=sub_len, dilation=dilation, slopes=slopes)
    q_spec = pl.BlockSpec((1, A_STEPS, A_GROUP), lambda b, c, n: (b, n, c * cols + group))
    k_spec = pl.BlockSpec((1, sub_len, A_GROUP), lambda b, c, n: (b, 0, c * cols + 3 + group))
    v_spec = pl.BlockSpec((1, sub_len, A_GROUP), lambda b, c, n: (b, 0, c * cols + 6 + group))
    o_spec = pl.BlockSpec((1, A_STEPS, A_GROUP), lambda b, c, n: (b, n, c))
    shape = jax.ShapeDtypeStruct((batch, sub_len, dilation * A_GROUP), F32)
    o, lse = pl.pallas_call(
        kern,
        grid=(batch, dilation, nb),
        in_specs=[q_spec, k_spec, v_spec],
        out_specs=[o_spec, o_spec],
        out_shape=[shape, shape],
        compiler_params=_params(("parallel", "parallel", "arbitrary")),
        name=f"dilated{group}",
    )(x, x, x)
    return o.reshape(batch * seq, A_GROUP), lse.reshape(batch * seq, A_GROUP)


ATT_TQ = 256
ATT_TK = 512


def _attn_kernel(*refs, diff, seq, lam_init):
    if diff:
        lam_ref, gsub_ref = refs[:2]
        refs = refs[2:]
    q_ref, k_ref, v_ref, o_ref, s_a, s_b = refs
    tq, tk = ATT_TQ, ATT_TK
    lane = lax.broadcasted_iota(jnp.int32, (1, LANES), 1)
    tri = lax.broadcasted_iota(jnp.int32, (tq, tq), 0) >= lax.broadcasted_iota(jnp.int32, (tq, tq), 1)
    if diff:
        h = pl.program_id(1)
        slope = jnp.exp2(jnp.full((1, 1), -2.0, F32) * (h + 1).astype(F32)) * LOG2E
        lam_rows = lam_ref[...]
        lam = (jnp.exp(jnp.sum(lam_rows[0:1] * lam_rows[1:2], axis=-1, keepdims=True))
               - jnp.exp(jnp.sum(lam_rows[2:3] * lam_rows[3:4], axis=-1, keepdims=True)) + lam_init)
    s_refs = (s_a, s_b)

    for qi in range(seq // tq):
        q0 = qi * tq
        q = q_ref[0, q0:q0 + tq, :]
        if diff:
            zero = jnp.zeros_like(q)
            qs = (jnp.where(lane < B_HEAD_DIM, q, zero), jnp.where(lane >= B_HEAD_DIM, q, zero))
        else:
            qs = (q[:, :C_PAD], q[:, C_PAD:])
        chunks = [(c * tk, tk, False) for c in range(q0 // tk)]
        if q0 % tk:
            chunks.append((q0 - q0 % tk, q0 % tk, False))
        chunks.append((q0, tq, True))

        m = [jnp.full((tq, LANES), NEG_INF, F32)] * 2
        for k0, w, masked in chunks:
            k = k_ref[0, k0:k0 + w, :]
            for idx in range(2):
                kk = k if diff else k[:, idx * C_PAD:(idx + 1) * C_PAD]
                s = _dot_nt(qs[idx], kk)
                if diff:
                    kpos = k0 - q0 + lax.broadcasted_iota(jnp.int32, (1, w), 1)
                    s = s + slope * kpos.astype(F32)
                if masked:
                    s = jnp.where(tri, s, NEG_INF)
                s_refs[idx][:, k0:k0 + w] = s
                for g in range(w // LANES):
                    m[idx] = jnp.maximum(m[idx], s[:, g * LANES:(g + 1) * LANES])
        row_max = [jnp.broadcast_to(jnp.max(mm, axis=-1, keepdims=True), (tq, LANES)) for mm in m]

        lsum = [jnp.zeros((tq, LANES), F32)] * 2
        acc = [jnp.zeros((tq, v_ref.shape[-1]), F32)] * 2
        for k0, w, _ in chunks:
            v = v_ref[0, k0:k0 + w, :]
            for idx in range(2):
                parts = []
                for g in range(w // LANES):
                    p = jnp.exp2(s_refs[idx][:, k0 + g * LANES:k0 + (g + 1) * LANES] - row_max[idx])
                    lsum[idx] = lsum[idx] + p
                    parts.append(p.astype(BF16))
                acc[idx] = acc[idx] + _dot(jnp.concatenate(parts, axis=1), v)
        oa = acc[0] / jnp.sum(lsum[0], axis=-1, keepdims=True)
        ob = acc[1] / jnp.sum(lsum[1], axis=-1, keepdims=True)
        if diff:
            out = _rms(oa - lam * ob, gsub_ref[...]) * (1.0 - lam_init)
        else:
            out = jnp.where(lane < C_V, oa, ob)
        o_ref[0, q0:q0 + tq, :] = out.astype(BF16)


def _attn_scratch(seq):
    return [pltpu.VMEM((ATT_TQ, seq), F32)] * 2


def _diff(qkvb, lam_rows, gsub, lam_init, batch, seq):
    x = qkvb.reshape(batch, seq, 2 * B_QK + B_V)
    kern = functools.partial(_attn_kernel, diff=True, seq=seq, lam_init=lam_init)
    out = pl.pallas_call(
        kern,
        grid=(batch, B_HEADS),
        in_specs=[_const_spec(lam_rows.shape), _const_spec(gsub.shape),
                  pl.BlockSpec((1, seq, LANES), lambda b, h: (b, 0, h)),
                  pl.BlockSpec((1, seq, LANES), lambda b, h: (b, 0, B_HEADS + h)),
                  pl.BlockSpec((1, seq, LANES), lambda b, h: (b, 0, 2 * B_HEADS + h))],
        out_specs=pl.BlockSpec((1, seq, LANES), lambda b, h: (b, 0, h)),
        out_shape=jax.ShapeDtypeStruct((batch, seq, B_V), BF16),
        scratch_shapes=_attn_scratch(seq),
        compiler_params=_params(("parallel", "parallel")),
        name="diff_attn",
    )(lam_rows, gsub, x, x, x)
    return out.reshape(batch * seq, B_V)


def _latent(qc, kc, vc, batch, seq):
    kern = functools.partial(_attn_kernel, diff=False, seq=seq, lam_init=0.0)
    out = pl.pallas_call(
        kern,
        grid=(batch, C_HEADS // 2),
        in_specs=[pl.BlockSpec((1, seq, 2 * C_PAD), lambda b, h: (b, 0, h)),
                  pl.BlockSpec((1, seq, 2 * C_PAD), lambda b, h: (b, 0, h)),
                  pl.BlockSpec((1, seq, 2 * C_V), lambda b, h: (b, 0, h))],
        out_specs=pl.BlockSpec((1, seq, 2 * C_V), lambda b, h: (b, 0, h)),
        out_shape=jax.ShapeDtypeStruct((batch, seq, C_HEADS * C_V), BF16),
        scratch_shapes=_attn_scratch(seq),
        compiler_params=_params(("parallel", "parallel")),
        name="latent_attn",
    )(qc.reshape(batch, seq, -1), kc.reshape(batch, seq, -1), vc.reshape(batch, seq, -1))
    return out.reshape(batch * seq, C_HEADS * C_V)


def _route_columns(sc, bs):
    gscore = []
    for g in range(N_EXPERT_GROUPS):
        a, b, c, d = bs[4 * g:4 * g + 4]
        hi1, lo1 = jnp.maximum(a, b), jnp.minimum(a, b)
        hi2, lo2 = jnp.maximum(c, d), jnp.minimum(c, d)
        gscore.append(jnp.maximum(hi1, hi2) + jnp.maximum(jnp.minimum(hi1, hi2), jnp.maximum(lo1, lo2)))
    picked = []
    for g in range(N_EXPERT_GROUPS):
        ok = None
        for i in range(N_EXPERT_GROUPS):
            if i == g:
                continue
            c = gscore[g] > gscore[i] if i < g else gscore[g] >= gscore[i]
            ok = c if ok is None else ok & c
        picked.append(ok)
    sel = []
    for g in range(N_EXPERT_GROUPS):
        for j in range(EXPERTS_PER_GROUP):
            vj = bs[4 * g + j]
            rank = jnp.zeros(vj.shape, jnp.int32)
            for i in range(EXPERTS_PER_GROUP):
                if i == j:
                    continue
                vi = bs[4 * g + i]
                ahead = vi >= vj if i < j else vi > vj
                rank = rank + ahead.astype(jnp.int32)
            sel.append(picked[g] & (rank < 2))
    picked_w = [jnp.where(sel[e], sc[e], 0.0) for e in range(N_EXPERTS)]
    total = picked_w[0]
    for e in range(1, N_EXPERTS):
        total = total + picked_w[e]
    return [w / total for w in picked_w]


def _outproj_kernel(oa0, la0, oa1, la1, oa2, la2, ob_ref, oc_ref, gt_ref, x_ref,
                    pa_ref, pb_ref, pc_ref, wo_ref, gffn_ref, wrh_ref, wrl_ref, rb_ref,
                    xn_ref, h_ref, comb_ref):
    l0, l1, l2 = la0[...], la1[...], la2[...]
    mx = jnp.maximum(jnp.maximum(l0, l1), l2)
    e0, e1, e2 = jnp.exp(l0 - mx), jnp.exp(l1 - mx), jnp.exp(l2 - mx)
    o_a = (e0 * oa0[...] + e1 * oa1[...] + e2 * oa2[...]) / (e0 + e1 + e2)
    ya = _dot(o_a.astype(BF16), pa_ref[...])
    yb = _dot(ob_ref[...], pb_ref[...])
    yc = _dot(oc_ref[...], pc_ref[...])
    merged = (gt_ref[:, 0:D_MODEL].astype(F32) * ya
              + gt_ref[:, D_MODEL:2 * D_MODEL].astype(F32) * yb
              + gt_ref[:, 2 * D_MODEL:3 * D_MODEL].astype(F32) * yc)
    xn = x_ref[...] + _dot(merged.astype(BF16), wo_ref[...])
    xn_ref[...] = xn
    hn = _rms(xn, gffn_ref[...])
    h_ref[...] = hn.astype(BF16)

    h_hi = hn.astype(BF16)
    h_lo = (hn - h_hi.astype(F32)).astype(BF16)
    logits = _dot(h_hi, wrh_ref[...]) + _dot(h_lo, wrh_ref[...]) + _dot(h_hi, wrl_ref[...])
    scores = _sigmoid(logits)
    biased = scores + rb_ref[...]
    sc = [scores[:, e:e + 1] for e in range(N_EXPERTS)]
    bs = [biased[:, e:e + 1] for e in range(N_EXPERTS)]
    comb_cols = _route_columns(sc, bs)
    lane = lax.broadcasted_iota(jnp.int32, (1, LANES), 1)
    comb = jnp.zeros(scores.shape, F32)
    for e in range(N_EXPERTS):
        comb = jnp.where(lane == e, comb_cols[e], comb)
    comb_ref[...] = comb


def _outproj(oas, las, ob, oc, gates, xf, pa, pb, pc, wo, gffn, wr_hi, wr_lo, rbias):
    T = xf.shape[0]
    tm = 256
    row = lambda w: pl.BlockSpec((tm, w), lambda i: (i, 0))
    ins, specs = [], []
    for o, l in zip(oas, las):
        ins += [o, l]
        specs += [row(A_GROUP), row(A_GROUP)]
    ins += [ob, oc, gates, xf, pa, pb, pc, wo, gffn, wr_hi, wr_lo, rbias]
    specs += [row(B_V), row(C_HEADS * C_V), row(3 * D_MODEL), row(D_MODEL)]
    specs += [_const_spec(a.shape) for a in (pa, pb, pc, wo, gffn, wr_hi, wr_lo, rbias)]
    return pl.pallas_call(
        _outproj_kernel,
        grid=(T // tm,),
        in_specs=specs,
        out_specs=[row(D_MODEL), row(D_MODEL), row(LANES)],
        out_shape=[jax.ShapeDtypeStruct((T, D_MODEL), F32), jax.ShapeDtypeStruct((T, D_MODEL), BF16),
                   jax.ShapeDtypeStruct((T, LANES), F32)],
        compiler_params=_params(("parallel",)),
        name="outproj",
    )(*ins)


def _experts_kernel(h_ref, comb_ref, x_ref, wg_ref, wu_ref, wd_ref, gfin_ref, o_ref, acc_ref, *, final):
    e = pl.program_id(1)

    @pl.when(e == 0)
    def _():
        acc_ref[...] = jnp.zeros(acc_ref.shape, F32)

    h = h_ref[...]
    a = _dot(h, wg_ref[0])
    b = _dot(h, wu_ref[0])
    lane = lax.broadcasted_iota(jnp.int32, (1, LANES), 1)
    c = jnp.sum(jnp.where(lane == e, comb_ref[...], 0.0), axis=-1, keepdims=True)
    act = a * _sigmoid(a) * b * c
    acc_ref[...] += _dot(act.astype(BF16), wd_ref[0])

    @pl.when(e == N_EXPERTS - 1)
    def _():
        y = x_ref[...] + acc_ref[...]
        if final:
            y = _rms(y, gfin_ref[...])
        o_ref[...] = y


def _experts(h, comb, xf, wg, wu, wd, gfin, final):
    T = xf.shape[0]
    tm = 1024
    kern = functools.partial(_experts_kernel, final=final)
    row = lambda w: pl.BlockSpec((tm, w), lambda i, e: (i, 0))
    return pl.pallas_call(
        kern,
        grid=(T // tm, N_EXPERTS),
        in_specs=[row(D_MODEL), row(LANES), row(D_MODEL),
                  pl.BlockSpec((1, D_MODEL, D_EXPERT), lambda i, e: (e, 0, 0)),
                  pl.BlockSpec((1, D_MODEL, D_EXPERT), lambda i, e: (e, 0, 0)),
                  pl.BlockSpec((1, D_EXPERT, D_MODEL), lambda i, e: (e, 0, 0)),
                  pl.BlockSpec((1, D_MODEL), lambda i, e: (0, 0))],
        out_specs=row(D_MODEL),
        out_shape=jax.ShapeDtypeStruct((T, D_MODEL), F32),
        scratch_shapes=[pltpu.VMEM((tm, D_MODEL), F32)],
        compiler_params=_params(("parallel", "arbitrary")),
        name="experts",
    )(h, comb, xf, wg, wu, wd, gfin)


def _alibi_slopes(n):
    return [2.0 ** (-8.0 * h / n) for h in range(1, n + 1)]


def _rope_tables(seq):
    half = C_ROPE // 2
    inv = ROPE_THETA ** (-jnp.arange(half, dtype=F32) * 2.0 / C_ROPE)
    ang = jnp.arange(seq, dtype=F32)[:, None] * inv[None, :]
    cos, sin = jnp.cos(ang), jnp.sin(ang)
    ones = jnp.ones((seq, C_NOPE), F32)
    zeros_n = jnp.zeros((seq, C_NOPE), F32)
    tail = jnp.zeros((seq, C_PAD - C_NOPE - C_ROPE), F32)
    ta = jnp.concatenate([ones, cos, cos, tail], axis=1)
    tb = jnp.concatenate([zeros_n, -sin, sin, tail], axis=1)
    return ta, tb


def _pack_in_weight(w):
    a_b_lat = w[:, 0:4224]
    kr = w[:, 4224:4256]
    gate = w[:, 4256:]
    half = C_ROPE // 2
    z = lambda n: jnp.zeros((w.shape[0], n), w.dtype)
    kr_main = jnp.concatenate([z(C_NOPE), kr, z(C_PAD - C_NOPE - C_ROPE)], axis=1)
    kr_swap = jnp.concatenate([z(C_NOPE), kr[:, half:], kr[:, :half], z(C_PAD - C_NOPE - C_ROPE)], axis=1)
    return jnp.concatenate([a_b_lat, kr_main, kr_swap, gate], axis=1).astype(BF16)


def _pack_latent_weights(w_uq, w_ukv):
    dq = C_NOPE + C_ROPE
    half = C_ROPE // 2
    rows_q = w_uq.shape[0]
    zq = lambda n: jnp.zeros((rows_q, n), w_uq.dtype)
    main, swap = [], []
    for h in range(C_HEADS):
        blk = w_uq[:, h * dq:(h + 1) * dq]
        main += [blk, zq(C_PAD - dq)]
        swap += [zq(C_NOPE), blk[:, C_NOPE + half:], blk[:, C_NOPE:C_NOPE + half], zq(C_PAD - dq)]
    wq2 = jnp.concatenate(main + swap, axis=1).astype(BF16)
    rows_k = w_ukv.shape[0]
    zk = jnp.zeros((rows_k, C_PAD - C_NOPE), w_ukv.dtype)
    dkv = C_NOPE + C_V
    kparts, vparts = [], []
    for h in range(C_HEADS):
        blk = w_ukv[:, h * dkv:(h + 1) * dkv]
        kparts += [blk[:, :C_NOPE], zk]
        vparts.append(blk[:, C_NOPE:])
    wkv2 = jnp.concatenate(kparts + vparts, axis=1).astype(BF16)
    return wq2, wkv2


def kernel(x, w_in, g_mix, p_a, p_b, p_c, w_o, g_q, w_uq, g_kv, w_ukv, lam_q1, lam_k1, lam_q2, lam_k2,
           g_sub, g_ffn, w_router, router_bias, w_gate, w_up, w_down, g_final):
    batch, seq, _ = x.shape
    xf = x.reshape(batch * seq, D_MODEL)
    ta, tb = _rope_tables(seq)
    slopes_a = _alibi_slopes(A_HEADS)
    wr = jnp.pad(w_router, ((0, 0), (0, LANES - N_EXPERTS)))
    wr_hi = wr.astype(BF16)
    wr_lo = (wr - wr_hi.astype(F32)).astype(BF16)
    rbias = jnp.pad(router_bias, (0, LANES - N_EXPERTS)).reshape(1, LANES)
    gfin = g_final.reshape(1, D_MODEL)

    for l in range(DEPTH):
        w_pack = _pack_in_weight(w_in[l])
        wq2, wkv2 = _pack_latent_weights(w_uq[l], w_ukv[l])
        qkva, qkvb, qc, kc, vc, gates = _inproj(
            xf, g_mix[l].reshape(1, -1), w_pack, g_q[l].reshape(1, -1), g_kv[l].reshape(1, -1),
            wq2, wkv2, ta, tb, seq)

        oas, las = [], []
        for g, (_, dilation) in enumerate(DIL_PAIRS):
            o, lse = _dilated(qkva, g, dilation, slopes_a[g * A_SLOTS:(g + 1) * A_SLOTS], batch, seq)
            oas.append(o)
            las.append(lse)

        lam_init = 0.8 - 0.6 * math.exp(-0.3 * l)
        lam_rows = jnp.stack([lam_q1[l], lam_k1[l], lam_q2[l], lam_k2[l]])
        ob = _diff(qkvb, lam_rows, g_sub[l].reshape(1, -1), lam_init, batch, seq)
        oc = _latent(qc, kc, vc, batch, seq)

        xf, h, comb = _outproj(oas, las, ob, oc, gates, xf,
                               p_a[l].astype(BF16), p_b[l].astype(BF16), p_c[l].astype(BF16),
                               w_o[l].astype(BF16), g_ffn[l].reshape(1, -1), wr_hi, wr_lo, rbias)
        xf = _experts(h, comb, xf, w_gate[l].astype(BF16), w_up[l].astype(BF16), w_down[l].astype(BF16),
                      gfin, final=(l == DEPTH - 1))
    return xf.reshape(batch, seq, D_MODEL)
```

```python
import functools
import math

import jax
import jax.numpy as jnp
import numpy as np
from jax import lax
from jax.experimental import pallas as pl
from jax.experimental.pallas import tpu as pltpu

F32 = jnp.float32
BF16 = jnp.bfloat16

D_MODEL = 1024
DEPTH = 2
DIL_PAIRS = ((128, 1), (512, 4), (2048, 16))
A_SLOTS = 4
A_HEAD_DIM = 64
A_GROUP = A_SLOTS * A_HEAD_DIM
A_QKV = 3 * A_GROUP
A_HEADS = 12
A_STEPS = 128
B_HEADS = 4
B_HEAD_DIM = 64
B_QK = 512
B_V = 512
C_HEADS = 6
C_NOPE = 64
C_ROPE = 32
C_V = 64
C_Q_LORA = 256
C_KV_LORA = 128
C_PAD = 128
ROPE_THETA = 10000.0
N_EXPERTS = 16
N_EXPERT_GROUPS = 4
EXPERTS_PER_GROUP = 4
D_EXPERT = 512
RMS_EPS = 1e-6
NEG_INF = -1e30

LANES = 128
LOG2E = 1.4426950408889634
VMEM_LIMIT = 56 * 1024 * 1024

COL_A = 0
COL_B = COL_A + 3 * A_QKV
COL_CQ = COL_B + 2 * B_QK + B_V
COL_CKV = COL_CQ + C_Q_LORA
COL_KRM = COL_CKV + C_KV_LORA
COL_KRS = COL_KRM + LANES
COL_GATE = COL_KRS + LANES
COL_END = COL_GATE + 3 * D_MODEL


def _rms(x, g):
    return x * lax.rsqrt(jnp.mean(x * x, axis=-1, keepdims=True) + RMS_EPS) * g


def _sigmoid(z):
    return 1.0 / (1.0 + jnp.exp(-z))


def _dot(a, b):
    return jnp.dot(a, b, preferred_element_type=F32)


def _dot_nt(a, b):
    return lax.dot_general(a, b, (((1,), (1,)), ((), ())), preferred_element_type=F32)


def _params(sem):
    return pltpu.CompilerParams(dimension_semantics=sem, vmem_limit_bytes=VMEM_LIMIT)


def _const_spec(shape):
    nd = len(shape)
    return pl.BlockSpec(shape, lambda *_: (0,) * nd, pipeline_mode=pl.Buffered(1))


def _inproj_kernel(x_ref, gmix_ref, w_ref, gq_ref, gkv_ref, wq2_ref, wkv2_ref, ta_ref, tb_ref,
                   oa_ref, ob_ref, qc_ref, kc_ref, vc_ref, og_ref):
    u = _rms(x_ref[...], gmix_ref[...]).astype(BF16)

    def mm(c0, c1):
        return _dot(u, w_ref[:, c0:c1])

    a_scale = A_HEAD_DIM ** -0.5
    oa_ref[:, 0:A_QKV] = (mm(COL_A, COL_A + A_QKV) * a_scale).astype(BF16)
    oa_ref[:, A_QKV:2 * A_QKV] = mm(COL_A + A_QKV, COL_A + 2 * A_QKV).astype(BF16)
    oa_ref[:, 2 * A_QKV:3 * A_QKV] = mm(COL_A + 2 * A_QKV, COL_B).astype(BF16)

    b_scale = B_HEAD_DIM ** -0.5 * LOG2E
    ob_ref[:, 0:B_QK] = (mm(COL_B, COL_B + B_QK) * b_scale).astype(BF16)
    ob_ref[:, B_QK:2 * B_QK] = mm(COL_B + B_QK, COL_B + 2 * B_QK).astype(BF16)
    ob_ref[:, 2 * B_QK:2 * B_QK + B_V] = mm(COL_B + 2 * B_QK, COL_CQ).astype(BF16)

    lat = mm(COL_CQ, COL_GATE)
    cqn = _rms(lat[:, 0:C_Q_LORA], gq_ref[...]).astype(BF16)
    ckvn = _rms(lat[:, C_Q_LORA:C_Q_LORA + C_KV_LORA], gkv_ref[...]).astype(BF16)
    krm = lat[:, COL_KRM - COL_CQ:COL_KRS - COL_CQ]
    krs = lat[:, COL_KRS - COL_CQ:COL_GATE - COL_CQ]
    q2 = _dot(cqn, wq2_ref[...])
    kv2 = _dot(ckvn, wkv2_ref[...])
    ta = ta_ref[...]
    tb = tb_ref[...]
    krot = krm * ta + krs * tb
    c_scale = (C_NOPE + C_ROPE) ** -0.5 * LOG2E
    nq = C_HEADS * C_PAD
    for h in range(C_HEADS):
        sl = slice(h * C_PAD, (h + 1) * C_PAD)
        qm = q2[:, h * C_PAD:(h + 1) * C_PAD]
        qs = q2[:, nq + h * C_PAD:nq + (h + 1) * C_PAD]
        qc_ref[:, sl] = ((qm * ta + qs * tb) * c_scale).astype(BF16)
        kc_ref[:, sl] = (kv2[:, h * C_PAD:(h + 1) * C_PAD] + krot).astype(BF16)
    vc_ref[...] = kv2[:, nq:nq + C_HEADS * C_V].astype(BF16)

    gchunk = 768
    for j in range(3 * D_MODEL // gchunk):
        z = mm(COL_GATE + j * gchunk, COL_GATE + (j + 1) * gchunk)
        og_ref[:, j * gchunk:(j + 1) * gchunk] = _sigmoid(z).astype(BF16)


def _inproj(xf, gmix, w_pack, gq, gkv, wq2, wkv2, ta, tb, seq):
    T = xf.shape[0]
    tm = 512
    n_pos_blocks = seq // tm
    row = lambda w: pl.BlockSpec((tm, w), lambda i: (i, 0))
    tab = pl.BlockSpec((tm, LANES), lambda i: (i % n_pos_blocks, 0))
    widths = (3 * A_QKV, 2 * B_QK + B_V, C_HEADS * C_PAD, C_HEADS * C_PAD, C_HEADS * C_V, 3 * D_MODEL)
    return pl.pallas_call(
        _inproj_kernel,
        grid=(T // tm,),
        in_specs=[row(D_MODEL), _const_spec((1, D_MODEL)), _const_spec(w_pack.shape),
                  _const_spec((1, C_Q_LORA)), _const_spec((1, C_KV_LORA)),
                  _const_spec(wq2.shape), _const_spec(wkv2.shape), tab, tab],
        out_specs=[row(w) for w in widths],
        out_shape=[jax.ShapeDtypeStruct((T, w), BF16) for w in widths],
        compiler_params=_params(("parallel",)),
        name="inproj",
    )(xf, gmix, w_pack, gq, gkv, wq2, wkv2, ta, tb)


def _dilated_kernel(q_ref, k_ref, v_ref, o_ref, l_ref, *, win, sub_len, dilation, slopes):
    n = pl.program_id(2)
    q = q_ref[0]
    start = pl.multiple_of(jnp.clip((n - 1) * A_STEPS, 0, sub_len - win), A_STEPS)
    k = k_ref[0, pl.ds(start, win), :]
    v = v_ref[0, pl.ds(start, win), :]
    qpos = n * A_STEPS + lax.broadcasted_iota(jnp.int32, (A_STEPS, 1), 0)
    kpos = start + lax.broadcasted_iota(jnp.int32, (1, win), 1)
    dist = qpos - kpos
    valid = (dist >= 0) & (dist <= A_STEPS)
    distf = dist.astype(F32)
    head_of_lane = lax.broadcasted_iota(jnp.int32, (1, A_GROUP), 1) // A_HEAD_DIM
    out = jnp.zeros((A_STEPS, A_GROUP), F32)
    lse = jnp.zeros((A_STEPS, A_GROUP), F32)
    for h in range(A_SLOTS):
        own = head_of_lane == h
        s = _dot_nt(jnp.where(own, q, jnp.zeros_like(q)), k)
        s = jnp.where(valid, s - (slopes[h] * dilation) * distf, NEG_INF)
        m = jnp.max(s, axis=-1, keepdims=True)
        p = jnp.exp(s - m)
        l = jnp.sum(p, axis=-1, keepdims=True)
        pv = _dot(p.astype(BF16), v)
        out = jnp.where(own, pv / l, out)
        lse = jnp.where(own, m + jnp.log(l), lse)
    o_ref[0] = out
    l_ref[0] = lse


def _dilated(qkva, group, dilation, slopes, batch, seq):
    sub_len = seq // dilation
    nb = sub_len // A_STEPS
    win = min(2 * A_STEPS, sub_len)
    cols = 3 * A_QKV // A_GROUP
    x = qkva.reshape(batch, sub_len, dilation * 3 * A_QKV)
    kern = functools.partial(_dilated_kernel, win=win, sub_len=sub_len, dilation=dilation, slopes=slopes)
    q_spec = pl.BlockSpec((1, A_STEPS, A_GROUP), lambda b, c, n: (b, n, c * cols + group))
    k_spec = pl.BlockSpec((1, sub_len, A_GROUP), lambda b, c, n: (b, 0, c * cols + 3 + group))
    v_spec = pl.BlockSpec((1, sub_len, A_GROUP), lambda b, c, n: (b, 0, c * cols + 6 + group))
    o_spec = pl.BlockSpec((1, A_STEPS, A_GROUP), lambda b, c, n: (b, n, c))
    shape = jax.ShapeDtypeStruct((batch, sub_len, dilation * A_GROUP), F32)
    o, lse = pl.pallas_call(
        kern,
        grid=(batch, dilation, nb),
        in_specs=[q_spec, k_spec, v_spec],
        out_specs=[o_spec, o_spec],
        out_shape=[shape, shape],
        compiler_params=_params(("parallel", "parallel", "arbitrary")),
        name=f"dilated{group}",
    )(x, x, x)
    return o.reshape(batch * seq, A_GROUP), lse.reshape(batch * seq, A_GROUP)


ATT_TQ = 256
ATT_TK = 512


def _attn_kernel(*refs, diff, seq, lam_init):
    if diff:
        lam_ref, gsub_ref = refs[:2]
        refs = refs[2:]
    q_ref, k_ref, v_ref, o_ref, s_a, s_b = refs
    tq, tk = ATT_TQ, ATT_TK
    lane = lax.broadcasted_iota(jnp.int32, (1, LANES), 1)
    tri = lax.broadcasted_iota(jnp.int32, (tq, tq), 0) >= lax.broadcasted_iota(jnp.int32, (tq, tq), 1)
    if diff:
        h = pl.program_id(1)
        slope = jnp.exp2(jnp.full((1, 1), -2.0, F32) * (h + 1).astype(F32)) * LOG2E
        lam_rows = lam_ref[...]
        lam = (jnp.exp(jnp.sum(lam_rows[0:1] * lam_rows[1:2], axis=-1, keepdims=True))
               - jnp.exp(jnp.sum(lam_rows[2:3] * lam_rows[3:4], axis=-1, keepdims=True)) + lam_init)
    s_refs = (s_a, s_b)

    for qi in range(seq // tq):
        q0 = qi * tq
        q = q_ref[0, q0:q0 + tq, :]
        if diff:
            zero = jnp.zeros_like(q)
            qs = (jnp.where(lane < B_HEAD_DIM, q, zero), jnp.where(lane >= B_HEAD_DIM, q, zero))
        else:
            qs = (q[:, :C_PAD], q[:, C_PAD:])
        chunks = [(c * tk, tk, False) for c in range(q0 // tk)]
        if q0 % tk:
            chunks.append((q0 - q0 % tk, q0 % tk, False))
        chunks.append((q0, tq, True))

        m = [jnp.full((tq, LANES), NEG_INF, F32)] * 2
        for k0, w, masked in chunks:
            k = k_ref[0, k0:k0 + w, :]
            for idx in range(2):
                kk = k if diff else k[:, idx * C_PAD:(idx + 1) * C_PAD]
                s = _dot_nt(qs[idx], kk)
                if diff:
                    kpos = k0 - q0 + lax.broadcasted_iota(jnp.int32, (1, w), 1)
                    s = s + slope * kpos.astype(F32)
                if masked:
                    s = jnp.where(tri, s, NEG_INF)
                s_refs[idx][:, k0:k0 + w] = s
                for g in range(w // LANES):
                    m[idx] = jnp.maximum(m[idx], s[:, g * LANES:(g + 1) * LANES])
        row_max = [jnp.broadcast_to(jnp.max(mm, axis=-1, keepdims=True), (tq, LANES)) for mm in m]

        lsum = [jnp.zeros((tq, LANES), F32)] * 2
        acc = [jnp.zeros((tq, v_ref.shape[-1]), F32)] * 2
        for k0, w, _ in chunks:
            v = v_ref[0, k0:k0 + w, :]
            for idx in range(2):
                parts = []
                for g in range(w // LANES):
                    p = jnp.exp2(s_refs[idx][:, k0 + g * LANES:k0 + (g + 1) * LANES] - row_max[idx])
                    lsum[idx] = lsum[idx] + p
                    parts.append(p.astype(BF16))
                acc[idx] = acc[idx] + _dot(jnp.concatenate(parts, axis=1), v)
        oa = acc[0] / jnp.sum(lsum[0], axis=-1, keepdims=True)
        ob = acc[1] / jnp.sum(lsum[1], axis=-1, keepdims=True)
        if diff:
            out = _rms(oa - lam * ob, gsub_ref[...]) * (1.0 - lam_init)
        else:
            out = jnp.where(lane < C_V, oa, ob)
        o_ref[0, q0:q0 + tq, :] = out.astype(BF16)


def _attn_scratch(seq):
    return [pltpu.VMEM((ATT_TQ, seq), F32)] * 2


def _diff(qkvb, lam_rows, gsub, lam_init, batch, seq):
    x = qkvb.reshape(batch, seq, 2 * B_QK + B_V)
    kern = functools.partial(_attn_kernel, diff=True, seq=seq, lam_init=lam_init)
    out = pl.pallas_call(
        kern,
        grid=(batch, B_HEADS),
        in_specs=[_const_spec(lam_rows.shape), _const_spec(gsub.shape),
                  pl.BlockSpec((1, seq, LANES), lambda b, h: (b, 0, h)),
                  pl.BlockSpec((1, seq, LANES), lambda b, h: (b, 0, B_HEADS + h)),
                  pl.BlockSpec((1, seq, LANES), lambda b, h: (b, 0, 2 * B_HEADS + h))],
        out_specs=pl.BlockSpec((1, seq, LANES), lambda b, h: (b, 0, h)),
        out_shape=jax.ShapeDtypeStruct((batch, seq, B_V), BF16),
        scratch_shapes=_attn_scratch(seq),
        compiler_params=_params(("parallel", "parallel")),
        name="diff_attn",
    )(lam_rows, gsub, x, x, x)
    return out.reshape(batch * seq, B_V)


def _latent(qc, kc, vc, batch, seq):
    kern = functools.partial(_attn_kernel, diff=False, seq=seq, lam_init=0.0)
    out = pl.pallas_call(
        kern,
        grid=(batch, C_HEADS // 2),
        in_specs=[pl.BlockSpec((1, seq, 2 * C_PAD), lambda b, h: (b, 0, h)),
                  pl.BlockSpec((1, seq, 2 * C_PAD), lambda b, h: (b, 0, h)),
                  pl.BlockSpec((1, seq, 2 * C_V), lambda b, h: (b, 0, h))],
        out_specs=pl.BlockSpec((1, seq, 2 * C_V), lambda b, h: (b, 0, h)),
        out_shape=jax.ShapeDtypeStruct((batch, seq, C_HEADS * C_V), BF16),
        scratch_shapes=_attn_scratch(seq),
        compiler_params=_params(("parallel", "parallel")),
        name="latent_attn",
    )(qc.reshape(batch, seq, -1), kc.reshape(batch, seq, -1), vc.reshape(batch, seq, -1))
    return out.reshape(batch * seq, C_HEADS * C_V)


TOK_TILE = 256
SLOT_TILE = 256
U32 = jnp.uint32


def _route(sc, bs):
    gscore = []
    for g in range(N_EXPERT_GROUPS):
        a, b, c, d = bs[4 * g:4 * g + 4]
        hi1, lo1 = jnp.maximum(a, b), jnp.minimum(a, b)
        hi2, lo2 = jnp.maximum(c, d), jnp.minimum(c, d)
        gscore.append(jnp.maximum(hi1, hi2) + jnp.maximum(jnp.minimum(hi1, hi2), jnp.maximum(lo1, lo2)))
    picked = []
    for g in range(N_EXPERT_GROUPS):
        ok = None
        for i in range(N_EXPERT_GROUPS):
            if i == g:
                continue
            c = gscore[g] > gscore[i] if i < g else gscore[g] >= gscore[i]
            ok = c if ok is None else ok & c
        picked.append(ok)
    sel = []
    for g in range(N_EXPERT_GROUPS):
        for j in range(EXPERTS_PER_GROUP):
            vj = bs[4 * g + j]
            rank = jnp.zeros(vj.shape, jnp.int32)
            for i in range(EXPERTS_PER_GROUP):
                if i == j:
                    continue
                vi = bs[4 * g + i]
                ahead = vi >= vj if i < j else vi > vj
                rank = rank + ahead.astype(jnp.int32)
            sel.append(picked[g] & (rank < 2))
    picked_w = [jnp.where(sel[e], sc[e], 0.0) for e in range(N_EXPERTS)]
    total = picked_w[0]
    for e in range(1, N_EXPERTS):
        total = total + picked_w[e]
    return sel, [w / total for w in picked_w]


def _outproj_kernel(oa0, la0, oa1, la1, oa2, la2, ob_ref, oc_ref, gt_ref, x_ref,
                    pa_ref, pb_ref, pc_ref, wo_ref, gffn_ref, wrh_ref, wrl_ref, rb_ref,
                    xn_ref, hp_ref, route_ref, wcol_ref, cnt_ref, base_ref):
    tm = TOK_TILE

    @pl.when(pl.program_id(0) == 0)
    def _():
        base_ref[...] = jnp.zeros(base_ref.shape, F32)

    l0, l1, l2 = la0[...], la1[...], la2[...]
    mx = jnp.maximum(jnp.maximum(l0, l1), l2)
    e0, e1, e2 = jnp.exp(l0 - mx), jnp.exp(l1 - mx), jnp.exp(l2 - mx)
    o_a = (e0 * oa0[...] + e1 * oa1[...] + e2 * oa2[...]) / (e0 + e1 + e2)
    ya = _dot(o_a.astype(BF16), pa_ref[...])
    yb = _dot(ob_ref[...], pb_ref[...])
    yc = _dot(oc_ref[...], pc_ref[...])
    merged = (gt_ref[:, 0:D_MODEL].astype(F32) * ya
              + gt_ref[:, D_MODEL:2 * D_MODEL].astype(F32) * yb
              + gt_ref[:, 2 * D_MODEL:3 * D_MODEL].astype(F32) * yc)
    xn = x_ref[...] + _dot(merged.astype(BF16), wo_ref[...])
    xn_ref[...] = xn
    hn = _rms(xn, gffn_ref[...])
    h_hi = hn.astype(BF16)
    h_lo = (hn - h_hi.astype(F32)).astype(BF16)
    bits = lax.bitcast_convert_type(h_hi.astype(F32), U32)
    half = D_MODEL // 2
    hp_ref[...] = (bits[:, :half] >> 16) | (bits[:, half:] & jnp.uint32(0xFFFF0000))

    logits = _dot_nt(wrh_ref[...], h_hi) + _dot_nt(wrh_ref[...], h_lo) + _dot_nt(wrl_ref[...], h_hi)
    scores = _sigmoid(logits)
    biased = scores + rb_ref[...]
    sc = [scores[e:e + 1, :] for e in range(N_EXPERTS)]
    bs = [biased[e:e + 1, :] for e in range(N_EXPERTS)]
    sel, w = _route(sc, bs)
    e0 = jnp.full((1, tm), N_EXPERTS, jnp.int32)
    e1 = jnp.full((1, tm), -1, jnp.int32)
    for e in range(N_EXPERTS):
        e0 = jnp.minimum(e0, jnp.where(sel[e], e, N_EXPERTS))
        e1 = jnp.maximum(e1, jnp.where(sel[e], e, -1))
    w0 = jnp.zeros((1, tm), F32)
    w1 = jnp.zeros((1, tm), F32)
    sub = lax.broadcasted_iota(jnp.int32, (N_EXPERTS, 1), 0)
    sel_m = jnp.zeros((N_EXPERTS, tm), F32)
    for e in range(N_EXPERTS):
        w0 = jnp.where(e0 == e, w[e], w0)
        w1 = jnp.where(e1 == e, w[e], w1)
        sel_m = jnp.where((sub == e) & sel[e], 1.0, sel_m)
    upper = lax.broadcasted_iota(jnp.int32, (tm, tm), 0) < lax.broadcasted_iota(jnp.int32, (tm, tm), 1)
    prefix = _dot(sel_m.astype(BF16), upper.astype(BF16)) + base_ref[:, 0:1]
    r0 = jnp.zeros((1, tm), F32)
    r1 = jnp.zeros((1, tm), F32)
    for e in range(N_EXPERTS):
        r0 = jnp.where(e0 == e, prefix[e:e + 1, :], r0)
        r1 = jnp.where(e1 == e, prefix[e:e + 1, :], r1)
    base_ref[...] = base_ref[...] + jnp.sum(sel_m, axis=1, keepdims=True)
    cnt_ref[...] = base_ref[...]
    zrow = jnp.zeros((1, tm), jnp.int32)
    route_ref[...] = jnp.concatenate(
        [e0, e1, r0.astype(jnp.int32), r1.astype(jnp.int32), zrow, zrow, zrow, zrow], axis=0)
    wcol_ref[...] = jnp.concatenate([w0, w1, jnp.zeros((LANES - 2, tm), F32)], axis=0).T


def _outproj(oas, las, ob, oc, gates, xf, pa, pb, pc, wo, gffn, wr_hi, wr_lo, rbias):
    T = xf.shape[0]
    tm = TOK_TILE
    row = lambda w: pl.BlockSpec((tm, w), lambda i: (i, 0))
    ins, specs = [], []
    for o, l in zip(oas, las):
        ins += [o, l]
        specs += [row(A_GROUP), row(A_GROUP)]
    ins += [ob, oc, gates, xf, pa, pb, pc, wo, gffn, wr_hi, wr_lo, rbias]
    specs += [row(B_V), row(C_HEADS * C_V), row(3 * D_MODEL), row(D_MODEL)]
    specs += [_const_spec(a.shape) for a in (pa, pb, pc, wo, gffn, wr_hi, wr_lo, rbias)]
    return pl.pallas_call(
        _outproj_kernel,
        grid=(T // tm,),
        in_specs=specs,
        out_specs=[row(D_MODEL), row(D_MODEL // 2), pl.BlockSpec((None, 8, tm), lambda i: (i, 0, 0)),
                   row(LANES), pl.BlockSpec((N_EXPERTS, LANES), lambda i: (0, 0))],
        out_shape=[jax.ShapeDtypeStruct((T, D_MODEL), F32), jax.ShapeDtypeStruct((T, D_MODEL // 2), U32),
                   jax.ShapeDtypeStruct((T // tm, 8, tm), jnp.int32), jax.ShapeDtypeStruct((T, LANES), F32),
                   jax.ShapeDtypeStruct((N_EXPERTS, LANES), F32)],
        scratch_shapes=[pltpu.VMEM((N_EXPERTS, LANES), F32)],
        compiler_params=_params(("arbitrary",)),
        name="outproj",
    )(*ins)


def _slot_plan(counts, n_tok):
    padded = (counts + SLOT_TILE - 1) // SLOT_TILE * SLOT_TILE
    ends = jnp.cumsum(padded)
    offsets = ends - padded
    max_tiles = (2 * n_tok) // SLOT_TILE + N_EXPERTS
    n_tiles = ends[-1:] // SLOT_TILE
    tile_start = jnp.minimum(jnp.arange(max_tiles, dtype=jnp.int32) * SLOT_TILE, ends[-1] - SLOT_TILE)
    tile_expert = jnp.sum((ends[None, :] <= tile_start[:, None]).astype(jnp.int32), axis=1)
    i32 = lambda a: a.astype(jnp.int32)
    return i32(offsets), i32(padded), i32(tile_expert), i32(n_tiles), max_tiles


def _dispatch_kernel(off_ref, pad_ref, nt_ref, route_ref, hp_ref, xs_ref, zero_ref, sem):
    tm = TOK_TILE
    max_tiles = xs_ref.shape[0] // SLOT_TILE

    def zero_tile(t):
        return pltpu.make_async_copy(
            zero_ref, xs_ref.at[pl.ds(pl.multiple_of(t * SLOT_TILE, SLOT_TILE), SLOT_TILE)], sem)

    @pl.when(pl.program_id(0) == 0)
    def _():
        zero_ref[...] = jnp.zeros(zero_ref.shape, U32)

        def fill(t, c):
            zero_tile(t).start()
            return c

        def drain(t, c):
            zero_tile(t).wait()
            return c

        lax.fori_loop(nt_ref[0], max_tiles, fill, 0)
        lax.fori_loop(nt_ref[0], max_tiles, drain, 0)
        for e in range(N_EXPERTS):
            @pl.when(pad_ref[e] > 0)
            def _():
                start = pl.multiple_of(off_ref[e] + pad_ref[e] - SLOT_TILE, SLOT_TILE)
                pltpu.make_async_copy(zero_ref, xs_ref.at[pl.ds(start, SLOT_TILE)], sem).start()
        for e in range(N_EXPERTS):
            @pl.when(pad_ref[e] > 0)
            def _():
                pltpu.make_async_copy(zero_ref, xs_ref.at[pl.ds(0, SLOT_TILE)], sem).wait()

    def issue(r, c):
        d0 = off_ref[route_ref[0, r]] + route_ref[2, r]
        d1 = off_ref[route_ref[1, r]] + route_ref[3, r]
        pltpu.make_async_copy(hp_ref.at[pl.ds(r, 1)], xs_ref.at[pl.ds(d0, 1)], sem).start()
        pltpu.make_async_copy(hp_ref.at[pl.ds(r, 1)], xs_ref.at[pl.ds(d1, 1)], sem).start()
        return c

    lax.fori_loop(0, tm, issue, 0, unroll=8)
    for _ in range(2):
        pltpu.make_async_copy(hp_ref, xs_ref.at[pl.ds(0, tm)], sem).wait()


def _dispatch(offsets, padded, n_tiles, route, hp, n_slots):
    tm = TOK_TILE
    half = D_MODEL // 2
    gs = pltpu.PrefetchScalarGridSpec(
        num_scalar_prefetch=3,
        grid=(route.shape[0],),
        in_specs=[pl.BlockSpec((None, 8, tm), lambda i, o, p, n: (i, 0, 0), memory_space=pltpu.SMEM),
                  pl.BlockSpec((tm, half), lambda i, o, p, n: (i, 0))],
        out_specs=pl.BlockSpec(memory_space=pl.ANY),
        scratch_shapes=[pltpu.VMEM((SLOT_TILE, half), U32), pltpu.SemaphoreType.DMA],
    )
    return pl.pallas_call(
        _dispatch_kernel, grid_spec=gs,
        out_shape=jax.ShapeDtypeStruct((n_slots, half), U32),
        compiler_params=_params(("arbitrary",)),
        name="dispatch",
    )(offsets, padded, n_tiles, route, hp)


def _ffn_kernel(te_ref, nt_ref, xs_ref, wg_ref, wu_ref, wd_ref, ys_ref):
    @pl.when(pl.program_id(0) >= nt_ref[0])
    def _():
        ys_ref[...] = jnp.zeros(ys_ref.shape, F32)

    @pl.when(pl.program_id(0) < nt_ref[0])
    def _():
        u = xs_ref[...]
        lo = lax.bitcast_convert_type(u << 16, F32).astype(BF16)
        hi = lax.bitcast_convert_type(u & jnp.uint32(0xFFFF0000), F32).astype(BF16)
        x = jnp.concatenate([lo, hi], axis=1)
        a = _dot(x, wg_ref[0])
        b = _dot(x, wu_ref[0])
        act = a * _sigmoid(a) * b
        ys_ref[...] = _dot(act.astype(BF16), wd_ref[0])


def _ffn(tile_expert, n_tiles, xs, wg, wu, wd, max_tiles):
    half = D_MODEL // 2
    used = lambda i, te, nt: (jnp.minimum(i, nt[0] - 1), 0)
    wspec = lambda shape: pl.BlockSpec((1,) + shape, lambda i, te, nt: (te[i], 0, 0))
    gs = pltpu.PrefetchScalarGridSpec(
        num_scalar_prefetch=2,
        grid=(max_tiles,),
        in_specs=[pl.BlockSpec((SLOT_TILE, half), used),
                  wspec((D_MODEL, D_EXPERT)), wspec((D_MODEL, D_EXPERT)), wspec((D_EXPERT, D_MODEL))],
        out_specs=pl.BlockSpec((SLOT_TILE, D_MODEL), lambda i, te, nt: (i, 0)),
    )
    return pl.pallas_call(
        _ffn_kernel, grid_spec=gs,
        out_shape=jax.ShapeDtypeStruct((xs.shape[0], D_MODEL), F32),
        compiler_params=_params(("arbitrary",)),
        name="expert_ffn",
    )(tile_expert, n_tiles, xs, wg, wu, wd)


def _combine_kernel(off_ref, route_ref, wcol_ref, x_ref, gfin_ref, ys_ref, o_ref, buf0, buf1, sem, *, final):
    tm = TOK_TILE

    def issue(r, c):
        d0 = off_ref[route_ref[0, r]] + route_ref[2, r]
        d1 = off_ref[route_ref[1, r]] + route_ref[3, r]
        pltpu.make_async_copy(ys_ref.at[pl.ds(d0, 1)], buf0.at[pl.ds(r, 1)], sem).start()
        pltpu.make_async_copy(ys_ref.at[pl.ds(d1, 1)], buf1.at[pl.ds(r, 1)], sem).start()
        return c

    lax.fori_loop(0, tm, issue, 0, unroll=8)
    for buf in (buf0, buf1):
        pltpu.make_async_copy(ys_ref.at[pl.ds(0, tm)], buf, sem).wait()
    w = wcol_ref[...]
    y = x_ref[...] + w[:, 0:1] * buf0[...] + w[:, 1:2] * buf1[...]
    if final:
        y = _rms(y, gfin_ref[...])
    o_ref[...] = y


def _combine(offsets, route, wcol, xf, gfin, ys, final):
    tm = TOK_TILE
    gs = pltpu.PrefetchScalarGridSpec(
        num_scalar_prefetch=1,
        grid=(route.shape[0],),
        in_specs=[pl.BlockSpec((None, 8, tm), lambda i, o: (i, 0, 0), memory_space=pltpu.SMEM),
                  pl.BlockSpec((tm, LANES), lambda i, o: (i, 0)),
                  pl.BlockSpec((tm, D_MODEL), lambda i, o: (i, 0)),
                  pl.BlockSpec((1, D_MODEL), lambda i, o: (0, 0)),
                  pl.BlockSpec(memory_space=pl.ANY)],
        out_specs=pl.BlockSpec((tm, D_MODEL), lambda i, o: (i, 0)),
        scratch_shapes=[pltpu.VMEM((tm, D_MODEL), F32), pltpu.VMEM((tm, D_MODEL), F32), pltpu.SemaphoreType.DMA],
    )
    return pl.pallas_call(
        functools.partial(_combine_kernel, final=final), grid_spec=gs,
        out_shape=jax.ShapeDtypeStruct(xf.shape, F32),
        compiler_params=_params(("arbitrary",)),
        name="combine",
    )(offsets, route, wcol, xf, gfin, ys)


def _alibi_slopes(n):
    return [2.0 ** (-8.0 * h / n) for h in range(1, n + 1)]


def _rope_tables(seq):
    half = C_ROPE // 2
    inv = ROPE_THETA ** (-jnp.arange(half, dtype=F32) * 2.0 / C_ROPE)
    ang = jnp.arange(seq, dtype=F32)[:, None] * inv[None, :]
    cos, sin = jnp.cos(ang), jnp.sin(ang)
    ones = jnp.ones((seq, C_NOPE), F32)
    zeros_n = jnp.zeros((seq, C_NOPE), F32)
    tail = jnp.zeros((seq, C_PAD - C_NOPE - C_ROPE), F32)
    ta = jnp.concatenate([ones, cos, cos, tail], axis=1)
    tb = jnp.concatenate([zeros_n, -sin, sin, tail], axis=1)
    return ta, tb


def _pack_in_weight(w):
    a_b_lat = w[:, 0:4224]
    kr = w[:, 4224:4256]
    gate = w[:, 4256:]
    half = C_ROPE // 2
    z = lambda n: jnp.zeros((w.shape[0], n), w.dtype)
    kr_main = jnp.concatenate([z(C_NOPE), kr, z(C_PAD - C_NOPE - C_ROPE)], axis=1)
    kr_swap = jnp.concatenate([z(C_NOPE), kr[:, half:], kr[:, :half], z(C_PAD - C_NOPE - C_ROPE)], axis=1)
    return jnp.concatenate([a_b_lat, kr_main, kr_swap, gate], axis=1).astype(BF16)


def _pack_latent_weights(w_uq, w_ukv):
    dq = C_NOPE + C_ROPE
    half = C_ROPE // 2
    rows_q = w_uq.shape[0]
    zq = lambda n: jnp.zeros((rows_q, n), w_uq.dtype)
    main, swap = [], []
    for h in range(C_HEADS):
        blk = w_uq[:, h * dq:(h + 1) * dq]
        main += [blk, zq(C_PAD - dq)]
        swap += [zq(C_NOPE), blk[:, C_NOPE + half:], blk[:, C_NOPE:C_NOPE + half], zq(C_PAD - dq)]
    wq2 = jnp.concatenate(main + swap, axis=1).astype(BF16)
    rows_k = w_ukv.shape[0]
    zk = jnp.zeros((rows_k, C_PAD - C_NOPE), w_ukv.dtype)
    dkv = C_NOPE + C_V
    kparts, vparts = [], []
    for h in range(C_HEADS):
        blk = w_ukv[:, h * dkv:(h + 1) * dkv]
        kparts += [blk[:, :C_NOPE], zk]
        vparts.append(blk[:, C_NOPE:])
    wkv2 = jnp.concatenate(kparts + vparts, axis=1).astype(BF16)
    return wq2, wkv2


def kernel(x, w_in, g_mix, p_a, p_b, p_c, w_o, g_q, w_uq, g_kv, w_ukv, lam_q1, lam_k1, lam_q2, lam_k2,
           g_sub, g_ffn, w_router, router_bias, w_gate, w_up, w_down, g_final):
    batch, seq, _ = x.shape
    xf = x.reshape(batch * seq, D_MODEL)
    ta, tb = _rope_tables(seq)
    slopes_a = _alibi_slopes(A_HEADS)
    wr = w_router.T
    wr_hi = wr.astype(BF16)
    wr_lo = (wr - wr_hi.astype(F32)).astype(BF16)
    rbias = router_bias.reshape(N_EXPERTS, 1)
    gfin = g_final.reshape(1, D_MODEL)

    for l in range(DEPTH):
        w_pack = _pack_in_weight(w_in[l])
        wq2, wkv2 = _pack_latent_weights(w_uq[l], w_ukv[l])
        qkva, qkvb, qc, kc, vc, gates = _inproj(
            xf, g_mix[l].reshape(1, -1), w_pack, g_q[l].reshape(1, -1), g_kv[l].reshape(1, -1),
            wq2, wkv2, ta, tb, seq)

        oas, las = [], []
        for g, (_, dilation) in enumerate(DIL_PAIRS):
            o, lse = _dilated(qkva, g, dilation, slopes_a[g * A_SLOTS:(g + 1) * A_SLOTS], batch, seq)
            oas.append(o)
            las.append(lse)

        lam_init = 0.8 - 0.6 * math.exp(-0.3 * l)
        lam_rows = jnp.stack([lam_q1[l], lam_k1[l], lam_q2[l], lam_k2[l]])
        ob = _diff(qkvb, lam_rows, g_sub[l].reshape(1, -1), lam_init, batch, seq)
        oc = _latent(qc, kc, vc, batch, seq)

        xf, hp, route, wcol, cnt = _outproj(oas, las, ob, oc, gates, xf,
                                            p_a[l].astype(BF16), p_b[l].astype(BF16), p_c[l].astype(BF16),
                                            w_o[l].astype(BF16), g_ffn[l].reshape(1, -1), wr_hi, wr_lo, rbias)
        offsets, padded, tile_expert, n_tiles, max_tiles = _slot_plan(cnt[:, 0].astype(jnp.int32), batch * seq)
        xs = _dispatch(offsets, padded, n_tiles, route, hp, max_tiles * SLOT_TILE)
        ys = _ffn(tile_expert, n_tiles, xs, w_gate[l].astype(BF16), w_up[l].astype(BF16), w_down[l].astype(BF16),
                  max_tiles)
        xf = _combine(offsets, route, wcol, xf, gfin, ys, final=(l == DEPTH - 1))
    return xf.reshape(batch, seq, D_MODEL)
```

```python
import functools
import math

import jax
import jax.numpy as jnp
import numpy as np
from jax import lax
from jax.experimental import pallas as pl
from jax.experimental.pallas import tpu as pltpu

F32 = jnp.float32
BF16 = jnp.bfloat16

D_MODEL = 1024
DEPTH = 2
DIL_PAIRS = ((128, 1), (512, 4), (2048, 16))
A_SLOTS = 4
A_HEAD_DIM = 64
A_GROUP = A_SLOTS * A_HEAD_DIM
A_QKV = 3 * A_GROUP
A_HEADS = 12
A_STEPS = 128
B_HEADS = 4
B_HEAD_DIM = 64
B_QK = 512
B_V = 512
C_HEADS = 6
C_NOPE = 64
C_ROPE = 32
C_V = 64
C_Q_LORA = 256
C_KV_LORA = 128
C_PAD = 128
ROPE_THETA = 10000.0
N_EXPERTS = 16
N_EXPERT_GROUPS = 4
EXPERTS_PER_GROUP = 4
D_EXPERT = 512
RMS_EPS = 1e-6
NEG_INF = -1e30

LANES = 128
LOG2E = 1.4426950408889634
VMEM_LIMIT = 56 * 1024 * 1024

COL_A = 0
COL_B = COL_A + 3 * A_QKV
COL_CQ = COL_B + 2 * B_QK + B_V
COL_CKV = COL_CQ + C_Q_LORA
COL_KRM = COL_CKV + C_KV_LORA
COL_KRS = COL_KRM + LANES
COL_GATE = COL_KRS + LANES
COL_END = COL_GATE + 3 * D_MODEL


def _rms(x, g):
    return x * lax.rsqrt(jnp.mean(x * x, axis=-1, keepdims=True) + RMS_EPS) * g


def _sigmoid(z):
    return 1.0 / (1.0 + jnp.exp(-z))


def _dot(a, b):
    return jnp.dot(a, b, preferred_element_type=F32)


def _dot_nt(a, b):
    return lax.dot_general(a, b, (((1,), (1,)), ((), ())), preferred_element_type=F32)


def _params(sem):
    return pltpu.CompilerParams(dimension_semantics=sem, vmem_limit_bytes=VMEM_LIMIT)


def _const_spec(shape):
    nd = len(shape)
    return pl.BlockSpec(shape, lambda *_: (0,) * nd, pipeline_mode=pl.Buffered(1))


def _inproj_kernel(x_ref, gmix_ref, w_ref, gq_ref, gkv_ref, wq2_ref, wkv2_ref, ta_ref, tb_ref,
                   oa_ref, ob_ref, qc_ref, kc_ref, vc_ref, og_ref):
    u = _rms(x_ref[...], gmix_ref[...]).astype(BF16)

    def mm(c0, c1):
        return _dot(u, w_ref[:, c0:c1])

    for comp in range(3):
        y = mm(COL_A + comp * A_QKV, COL_A + (comp + 1) * A_QKV)
        if comp == 0:
            y = y * A_HEAD_DIM ** -0.5
        for g in range(len(DIL_PAIRS)):
            for hf in range(2):
                c0 = g * A_GROUP + hf * LANES
                oa_ref[(g * 3 + comp) * 2 + hf] = y[:, c0:c0 + LANES]

    b_scale = B_HEAD_DIM ** -0.5 * LOG2E
    ob_ref[:, 0:B_QK] = (mm(COL_B, COL_B + B_QK) * b_scale).astype(BF16)
    ob_ref[:, B_QK:2 * B_QK] = mm(COL_B + B_QK, COL_B + 2 * B_QK).astype(BF16)
    ob_ref[:, 2 * B_QK:2 * B_QK + B_V] = mm(COL_B + 2 * B_QK, COL_CQ).astype(BF16)

    lat = mm(COL_CQ, COL_GATE)
    cqn = _rms(lat[:, 0:C_Q_LORA], gq_ref[...]).astype(BF16)
    ckvn = _rms(lat[:, C_Q_LORA:C_Q_LORA + C_KV_LORA], gkv_ref[...]).astype(BF16)
    krm = lat[:, COL_KRM - COL_CQ:COL_KRS - COL_CQ]
    krs = lat[:, COL_KRS - COL_CQ:COL_GATE - COL_CQ]
    q2 = _dot(cqn, wq2_ref[...])
    kv2 = _dot(ckvn, wkv2_ref[...])
    ta = ta_ref[...]
    tb = tb_ref[...]
    krot = krm * ta + krs * tb
    c_scale = (C_NOPE + C_ROPE) ** -0.5 * LOG2E
    nq = C_HEADS * C_PAD
    for h in range(C_HEADS):
        sl = slice(h * C_PAD, (h + 1) * C_PAD)
        qm = q2[:, h * C_PAD:(h + 1) * C_PAD]
        qs = q2[:, nq + h * C_PAD:nq + (h + 1) * C_PAD]
        qc_ref[:, sl] = ((qm * ta + qs * tb) * c_scale).astype(BF16)
        kc_ref[:, sl] = (kv2[:, h * C_PAD:(h + 1) * C_PAD] + krot).astype(BF16)
    vc_ref[...] = kv2[:, nq:nq + C_HEADS * C_V].astype(BF16)

    gchunk = 768
    for j in range(3 * D_MODEL // gchunk):
        z = mm(COL_GATE + j * gchunk, COL_GATE + (j + 1) * gchunk)
        og_ref[:, j * gchunk:(j + 1) * gchunk] = _sigmoid(z).astype(BF16)


def _inproj(xf, gmix, w_pack, gq, gkv, wq2, wkv2, ta, tb, seq):
    T = xf.shape[0]
    tm = 512
    n_pos_blocks = seq // tm
    row = lambda w: pl.BlockSpec((tm, w), lambda i: (i, 0))
    tab = pl.BlockSpec((tm, LANES), lambda i: (i % n_pos_blocks, 0))
    widths = (2 * B_QK + B_V, C_HEADS * C_PAD, C_HEADS * C_PAD, C_HEADS * C_V, 3 * D_MODEL)
    a_slabs = 3 * A_QKV // LANES
    return pl.pallas_call(
        _inproj_kernel,
        grid=(T // tm,),
        in_specs=[row(D_MODEL), _const_spec((1, D_MODEL)), _const_spec(w_pack.shape),
                  _const_spec((1, C_Q_LORA)), _const_spec((1, C_KV_LORA)),
                  _const_spec(wq2.shape), _const_spec(wkv2.shape), tab, tab],
        out_specs=[pl.BlockSpec((a_slabs, tm, LANES), lambda i: (0, i, 0))] + [row(w) for w in widths],
        out_shape=[jax.ShapeDtypeStruct((a_slabs, T, LANES), F32)]
        + [jax.ShapeDtypeStruct((T, w), BF16) for w in widths],
        compiler_params=_params(("parallel",)),
        name="inproj",
    )(xf, gmix, w_pack, gq, gkv, wq2, wkv2, ta, tb)


def _dilated_kernel(a_ref, o_ref, l_ref, *, dilation, sub_len, slopes):
    d = dilation
    nb = sub_len // A_STEPS
    win = min(2 * A_STEPS, sub_len)
    stack = A_SLOTS * A_STEPS
    head_lane = lax.broadcasted_iota(jnp.int32, (1, A_GROUP), 1) // A_HEAD_DIM
    row = lax.broadcasted_iota(jnp.int32, (stack, 1), 0)
    slope_col = jnp.zeros((stack, 1), F32)
    for h in range(A_SLOTS):
        slope_col = jnp.where(row // A_STEPS == h, slopes[h] * d, slope_col)
    rel = row % A_STEPS - lax.broadcasted_iota(jnp.int32, (1, win), 1)

    def bias_mask(offset):
        dist = rel + offset
        return jnp.where((dist >= 0) & (dist <= A_STEPS), -slope_col * dist.astype(F32), NEG_INF)

    bm_first = bias_mask(0)
    bm_rest = bias_mask(A_STEPS) if nb > 1 else None

    def rows(first, count):
        return pl.ds(first, count) if d == 1 else pl.ds(first, count, stride=d)

    def load(pair, rws):
        return jnp.concatenate([a_ref[2 * pair, rws, :], a_ref[2 * pair + 1, rws, :]], axis=1).astype(BF16)

    for c in range(d):
        for n in range(nb):
            start = min(max((n - 1) * A_STEPS, 0), sub_len - win)
            bm = bm_first if n * A_STEPS == start else bm_rest
            q_rows = rows(c + n * A_STEPS * d, A_STEPS)
            k_rows = rows(c + start * d, win)
            q = load(0, q_rows)
            k = load(1, k_rows)
            v = load(2, k_rows)
            zero = jnp.zeros_like(q)
            q4 = jnp.concatenate([jnp.where(head_lane == h, q, zero) for h in range(A_SLOTS)], axis=0)
            s = _dot_nt(q4, k) + bm
            m = jnp.max(s, axis=-1, keepdims=True)
            p = jnp.exp(s - m)
            l = jnp.sum(p, axis=-1, keepdims=True)
            o_st = _dot(p.astype(BF16), v) / l
            lse_st = m + jnp.log(l)
            out = jnp.zeros((A_STEPS, A_GROUP), F32)
            lse = jnp.zeros((A_STEPS, A_GROUP), F32)
            for h in range(A_SLOTS):
                own = head_lane == h
                out = jnp.where(own, o_st[h * A_STEPS:(h + 1) * A_STEPS], out)
                lse = jnp.where(own, lse_st[h * A_STEPS:(h + 1) * A_STEPS], lse)
            for hf in range(2):
                o_ref[hf, q_rows, :] = out[:, hf * LANES:(hf + 1) * LANES]
                l_ref[hf, q_rows, :] = lse[:, hf * LANES:(hf + 1) * LANES]


def _dilated(a_slabs, group, dilation, slopes, batch, seq):
    kern = functools.partial(_dilated_kernel, dilation=dilation, sub_len=seq // dilation, slopes=slopes)
    o_spec = pl.BlockSpec((2, seq, LANES), lambda b: (0, b, 0))
    shape = jax.ShapeDtypeStruct((2, batch * seq, LANES), F32)
    return pl.pallas_call(
        kern,
        grid=(batch,),
        in_specs=[pl.BlockSpec((6, seq, LANES), lambda b: (group, b, 0))],
        out_specs=[o_spec, o_spec],
        out_shape=[shape, shape],
        compiler_params=_params(("parallel",)),
        name=f"dilated{group}",
    )(a_slabs)


ATT_TQ = 256
ATT_TK = 512


def _attn_kernel(*refs, diff, seq, lam_init):
    if diff:
        lam_ref, gsub_ref = refs[:2]
        refs = refs[2:]
    q_ref, k_ref, v_ref, o_ref, s_a, s_b = refs
    tq, tk = ATT_TQ, ATT_TK
    lane = lax.broadcasted_iota(jnp.int32, (1, LANES), 1)
    tri = lax.broadcasted_iota(jnp.int32, (tq, tq), 0) >= lax.broadcasted_iota(jnp.int32, (tq, tq), 1)
    if diff:
        h = pl.program_id(1)
        slope = jnp.exp2(jnp.full((1, 1), -2.0, F32) * (h + 1).astype(F32)) * LOG2E
        lam_rows = lam_ref[...]
        lam = (jnp.exp(jnp.sum(lam_rows[0:1] * lam_rows[1:2], axis=-1, keepdims=True))
               - jnp.exp(jnp.sum(lam_rows[2:3] * lam_rows[3:4], axis=-1, keepdims=True)) + lam_init)
    s_refs = (s_a, s_b)

    for qi in range(seq // tq):
        q0 = qi * tq
        q = q_ref[0, q0:q0 + tq, :]
        if diff:
            zero = jnp.zeros_like(q)
            qs = (jnp.where(lane < B_HEAD_DIM, q, zero), jnp.where(lane >= B_HEAD_DIM, q, zero))
        else:
            qs = (q[:, :C_PAD], q[:, C_PAD:])
        chunks = [(c * tk, tk, False) for c in range(q0 // tk)]
        if q0 % tk:
            chunks.append((q0 - q0 % tk, q0 % tk, False))
        chunks.append((q0, tq, True))

        m = [jnp.full((tq, LANES), NEG_INF, F32)] * 2
        for k0, w, masked in chunks:
            k = k_ref[0, k0:k0 + w, :]
            for idx in range(2):
                kk = k if diff else k[:, idx * C_PAD:(idx + 1) * C_PAD]
                s = _dot_nt(qs[idx], kk)
                if diff:
                    kpos = k0 - q0 + lax.broadcasted_iota(jnp.int32, (1, w), 1)
                    s = s + slope * kpos.astype(F32)
                if masked:
                    s = jnp.where(tri, s, NEG_INF)
                s_refs[idx][:, k0:k0 + w] = s
                for g in range(w // LANES):
                    m[idx] = jnp.maximum(m[idx], s[:, g * LANES:(g + 1) * LANES])
        row_max = [jnp.broadcast_to(jnp.max(mm, axis=-1, keepdims=True), (tq, LANES)) for mm in m]

        lsum = [jnp.zeros((tq, LANES), F32)] * 2
        acc = [jnp.zeros((tq, v_ref.shape[-1]), F32)] * 2
        for k0, w, _ in chunks:
            v = v_ref[0, k0:k0 + w, :]
            for idx in range(2):
                parts = []
                for g in range(w // LANES):
                    p = jnp.exp2(s_refs[idx][:, k0 + g * LANES:k0 + (g + 1) * LANES] - row_max[idx])
                    lsum[idx] = lsum[idx] + p
                    parts.append(p.astype(BF16))
                acc[idx] = acc[idx] + _dot(jnp.concatenate(parts, axis=1), v)
        oa = acc[0] / jnp.sum(lsum[0], axis=-1, keepdims=True)
        ob = acc[1] / jnp.sum(lsum[1], axis=-1, keepdims=True)
        if diff:
            out = _rms(oa - lam * ob, gsub_ref[...]) * (1.0 - lam_init)
        else:
            out = jnp.where(lane < C_V, oa, ob)
        o_ref[0, q0:q0 + tq, :] = out.astype(BF16)


def _attn_scratch(seq):
    return [pltpu.VMEM((ATT_TQ, seq), F32)] * 2


def _diff(qkvb, lam_rows, gsub, lam_init, batch, seq):
    x = qkvb.reshape(batch, seq, 2 * B_QK + B_V)
    kern = functools.partial(_attn_kernel, diff=True, seq=seq, lam_init=lam_init)
    out = pl.pallas_call(
        kern,
        grid=(batch, B_HEADS),
        in_specs=[_const_spec(lam_rows.shape), _const_spec(gsub.shape),
                  pl.BlockSpec((1, seq, LANES), lambda b, h: (b, 0, h)),
                  pl.BlockSpec((1, seq, LANES), lambda b, h: (b, 0, B_HEADS + h)),
                  pl.BlockSpec((1, seq, LANES), lambda b, h: (b, 0, 2 * B_HEADS + h))],
        out_specs=pl.BlockSpec((1, seq, LANES), lambda b, h: (b, 0, h)),
        out_shape=jax.ShapeDtypeStruct((batch, seq, B_V), BF16),
        scratch_shapes=_attn_scratch(seq),
        compiler_params=_params(("parallel", "parallel")),
        name="diff_attn",
    )(lam_rows, gsub, x, x, x)
    return out.reshape(batch * seq, B_V)


def _latent(qc, kc, vc, batch, seq):
    kern = functools.partial(_attn_kernel, diff=False, seq=seq, lam_init=0.0)
    out = pl.pallas_call(
        kern,
        grid=(batch, C_HEADS // 2),
        in_specs=[pl.BlockSpec((1, seq, 2 * C_PAD), lambda b, h: (b, 0, h)),
                  pl.BlockSpec((1, seq, 2 * C_PAD), lambda b, h: (b, 0, h)),
                  pl.BlockSpec((1, seq, 2 * C_V), lambda b, h: (b, 0, h))],
        out_specs=pl.BlockSpec((1, seq, 2 * C_V), lambda b, h: (b, 0, h)),
        out_shape=jax.ShapeDtypeStruct((batch, seq, C_HEADS * C_V), BF16),
        scratch_shapes=_attn_scratch(seq),
        compiler_params=_params(("parallel", "parallel")),
        name="latent_attn",
    )(qc.reshape(batch, seq, -1), kc.reshape(batch, seq, -1), vc.reshape(batch, seq, -1))
    return out.reshape(batch * seq, C_HEADS * C_V)


TOK_TILE = 256
SLOT_TILE = 256
ROW_SUB = 8
ISSUE_GROUP = 8


def _route(sc, bs):
    gscore = []
    for g in range(N_EXPERT_GROUPS):
        a, b, c, d = bs[4 * g:4 * g + 4]
        hi1, lo1 = jnp.maximum(a, b), jnp.minimum(a, b)
        hi2, lo2 = jnp.maximum(c, d), jnp.minimum(c, d)
        gscore.append(jnp.maximum(hi1, hi2) + jnp.maximum(jnp.minimum(hi1, hi2), jnp.maximum(lo1, lo2)))
    picked = []
    for g in range(N_EXPERT_GROUPS):
        ok = None
        for i in range(N_EXPERT_GROUPS):
            if i == g:
                continue
            c = gscore[g] > gscore[i] if i < g else gscore[g] >= gscore[i]
            ok = c if ok is None else ok & c
        picked.append(ok)
    sel = []
    for g in range(N_EXPERT_GROUPS):
        for j in range(EXPERTS_PER_GROUP):
            vj = bs[4 * g + j]
            rank = jnp.zeros(vj.shape, jnp.int32)
            for i in range(EXPERTS_PER_GROUP):
                if i == j:
                    continue
                vi = bs[4 * g + i]
                ahead = vi >= vj if i < j else vi > vj
                rank = rank + ahead.astype(jnp.int32)
            sel.append(picked[g] & (rank < 2))
    picked_w = [jnp.where(sel[e], sc[e], 0.0) for e in range(N_EXPERTS)]
    total = picked_w[0]
    for e in range(1, N_EXPERTS):
        total = total + picked_w[e]
    return sel, [w / total for w in picked_w]


def _outproj_kernel(oa0, la0, oa1, la1, oa2, la2, ob_ref, oc_ref, gt_ref, x_ref,
                    pa_ref, pb_ref, pc_ref, wo_ref, gffn_ref, wrh_ref, wrl_ref, rb_ref,
                    xn_ref, hrow_ref, dest_ref, wcol_ref, cnt_ref, base_ref, route_ref):
    tm = TOK_TILE
    step = pl.program_id(0)

    @pl.when(step == 0)
    def _():
        base_ref[...] = jnp.zeros(base_ref.shape, F32)

    wide = lambda ref: jnp.concatenate([ref[0], ref[1]], axis=1)
    l0, l1, l2 = wide(la0), wide(la1), wide(la2)
    mx = jnp.maximum(jnp.maximum(l0, l1), l2)
    g0, g1, g2 = jnp.exp(l0 - mx), jnp.exp(l1 - mx), jnp.exp(l2 - mx)
    o_a = (g0 * wide(oa0) + g1 * wide(oa1) + g2 * wide(oa2)) / (g0 + g1 + g2)
    ya = _dot(o_a.astype(BF16), pa_ref[...])
    yb = _dot(ob_ref[...], pb_ref[...])
    yc = _dot(oc_ref[...], pc_ref[...])
    merged = (gt_ref[:, 0:D_MODEL].astype(F32) * ya
              + gt_ref[:, D_MODEL:2 * D_MODEL].astype(F32) * yb
              + gt_ref[:, 2 * D_MODEL:3 * D_MODEL].astype(F32) * yc)
    xn = x_ref[...] + _dot(merged.astype(BF16), wo_ref[...])
    xn_ref[...] = xn
    hn = _rms(xn, gffn_ref[...])
    h_hi = hn.astype(BF16)
    h_lo = (hn - h_hi.astype(F32)).astype(BF16)
    _to_row_tiles(hrow_ref, hn)

    logits = _dot_nt(wrh_ref[...], h_hi) + _dot_nt(wrh_ref[...], h_lo) + _dot_nt(wrl_ref[...], h_hi)
    scores = _sigmoid(logits)
    biased = scores + rb_ref[...]
    sc = [scores[e:e + 1, :] for e in range(N_EXPERTS)]
    bs = [biased[e:e + 1, :] for e in range(N_EXPERTS)]
    sel, w = _route(sc, bs)
    e0 = jnp.full((1, tm), N_EXPERTS, jnp.int32)
    e1 = jnp.full((1, tm), -1, jnp.int32)
    for e in range(N_EXPERTS):
        e0 = jnp.minimum(e0, jnp.where(sel[e], e, N_EXPERTS))
        e1 = jnp.maximum(e1, jnp.where(sel[e], e, -1))
    w0 = jnp.zeros((1, tm), F32)
    w1 = jnp.zeros((1, tm), F32)
    sub = lax.broadcasted_iota(jnp.int32, (N_EXPERTS, 1), 0)
    sel_m = jnp.zeros((N_EXPERTS, tm), F32)
    for e in range(N_EXPERTS):
        w0 = jnp.where(e0 == e, w[e], w0)
        w1 = jnp.where(e1 == e, w[e], w1)
        sel_m = jnp.where((sub == e) & sel[e], 1.0, sel_m)
    upper = lax.broadcasted_iota(jnp.int32, (tm, tm), 0) < lax.broadcasted_iota(jnp.int32, (tm, tm), 1)
    prefix = _dot(sel_m.astype(BF16), upper.astype(BF16)) + base_ref[:, 0:1]
    r0 = jnp.zeros((1, tm), F32)
    r1 = jnp.zeros((1, tm), F32)
    for e in range(N_EXPERTS):
        r0 = jnp.where(e0 == e, prefix[e:e + 1, :], r0)
        r1 = jnp.where(e1 == e, prefix[e:e + 1, :], r1)
    base_ref[...] = base_ref[...] + jnp.sum(sel_m, axis=1, keepdims=True)
    cnt_ref[...] = base_ref[...]
    zrow = jnp.zeros((1, tm), jnp.int32)
    route_ref[step] = jnp.concatenate(
        [e0, e1, r0.astype(jnp.int32), r1.astype(jnp.int32), zrow, zrow, zrow, zrow], axis=0)

    @pl.when(step == pl.num_programs(0) - 1)
    def _():
        counts = base_ref[:, 0:1]
        padded = jnp.ceil(counts * (1.0 / SLOT_TILE)) * SLOT_TILE
        routes = route_ref[...]
        ea, eb = routes[:, 0, :], routes[:, 1, :]
        da, db = routes[:, 2, :], routes[:, 3, :]
        start = jnp.zeros((1, 1), F32)
        for e in range(N_EXPERTS):
            s_e = start.astype(jnp.int32)
            da = da + jnp.where(ea == e, s_e, 0)
            db = db + jnp.where(eb == e, s_e, 0)
            start = start + padded[e:e + 1, :]
        dest_ref[:, 0, :] = jnp.concatenate([da, db], axis=1)

    wcol_ref[...] = jnp.concatenate([w0, w1, jnp.zeros((LANES - 2, tm), F32)], axis=0).T


def _outproj(oas, las, ob, oc, gates, xf, pa, pb, pc, wo, gffn, wr_hi, wr_lo, rbias):
    T = xf.shape[0]
    tm = TOK_TILE
    row = lambda w: pl.BlockSpec((tm, w), lambda i: (i, 0))
    ins, specs = [], []
    for o, l in zip(oas, las):
        ins += [o, l]
        specs += [pl.BlockSpec((2, tm, LANES), lambda i: (0, i, 0))] * 2
    ins += [ob, oc, gates, xf, pa, pb, pc, wo, gffn, wr_hi, wr_lo, rbias]
    specs += [row(B_V), row(C_HEADS * C_V), row(3 * D_MODEL), row(D_MODEL)]
    specs += [_const_spec(a.shape) for a in (pa, pb, pc, wo, gffn, wr_hi, wr_lo, rbias)]
    return pl.pallas_call(
        _outproj_kernel,
        grid=(T // tm,),
        in_specs=specs,
        out_specs=[row(D_MODEL), pl.BlockSpec((tm * ROW_SUB, LANES), lambda i: (i, 0)),
                   pl.BlockSpec((T // tm, 1, 2 * tm), lambda i: (0, 0, 0)),
                   row(LANES), pl.BlockSpec((N_EXPERTS, LANES), lambda i: (0, 0))],
        out_shape=[jax.ShapeDtypeStruct((T, D_MODEL), F32), jax.ShapeDtypeStruct((T * ROW_SUB, LANES), F32),
                   jax.ShapeDtypeStruct((T // tm, 1, 2 * tm), jnp.int32), jax.ShapeDtypeStruct((T, LANES), F32),
                   jax.ShapeDtypeStruct((N_EXPERTS, LANES), F32)],
        scratch_shapes=[pltpu.VMEM((N_EXPERTS, LANES), F32), pltpu.VMEM((T // tm, 8, tm), jnp.int32)],
        compiler_params=_params(("arbitrary",)),
        name="outproj",
    )(*ins)


def _slot_plan(counts, n_tok):
    padded = (counts + SLOT_TILE - 1) // SLOT_TILE * SLOT_TILE
    ends = jnp.cumsum(padded)
    offsets = ends - padded
    max_tiles = (2 * n_tok) // SLOT_TILE + N_EXPERTS
    n_tiles = ends[-1:] // SLOT_TILE
    tile_start = jnp.minimum(jnp.arange(max_tiles, dtype=jnp.int32) * SLOT_TILE, ends[-1] - SLOT_TILE)
    tile_expert = jnp.sum((ends[None, :] <= tile_start[:, None]).astype(jnp.int32), axis=1)
    i32 = lambda a: a.astype(jnp.int32)
    return i32(offsets), i32(padded), i32(tile_expert), i32(n_tiles), max_tiles


def _to_row_tiles(ref, x):
    n = x.shape[0]
    for j in range(ROW_SUB):
        ref[pl.ds(j, n, stride=ROW_SUB), :] = x[:, j * LANES:(j + 1) * LANES]


def _from_row_tiles(ref, n):
    return jnp.concatenate([ref[pl.ds(j, n, stride=ROW_SUB), :] for j in range(ROW_SUB)], axis=1)


def _row_tile(ref, r):
    return ref.at[pl.ds(pl.multiple_of(r * ROW_SUB, ROW_SUB), ROW_SUB)]


def _dispatch_kernel(off_ref, pad_ref, nt_ref, dest_ref, hrow_ref, xs_ref, zero_ref, sem, zsem):
    tm = TOK_TILE
    i = pl.program_id(0)
    tile_rows = SLOT_TILE * ROW_SUB
    max_tiles = xs_ref.shape[0] // tile_rows

    def zero_tile(t):
        return pltpu.make_async_copy(
            zero_ref, xs_ref.at[pl.ds(pl.multiple_of(t * tile_rows, tile_rows), tile_rows)], zsem)

    @pl.when(i == 0)
    def _():
        zero_ref[...] = jnp.zeros(zero_ref.shape, F32)

        def fill(t, c):
            zero_tile(t).start()
            return c

        def drain(t, c):
            zero_tile(t).wait()
            return c

        lax.fori_loop(nt_ref[0], max_tiles, fill, 0)
        lax.fori_loop(nt_ref[0], max_tiles, drain, 0)
        for e in range(N_EXPERTS):
            @pl.when(pad_ref[e] > 0)
            def _():
                zero_tile((off_ref[e] + pad_ref[e]) // SLOT_TILE - 1).start()
        for e in range(N_EXPERTS):
            @pl.when(pad_ref[e] > 0)
            def _():
                zero_tile(0).wait()

    def issue(g, c):
        rows = [g * ISSUE_GROUP + u for u in range(ISSUE_GROUP)]
        slots = [(dest_ref[r], dest_ref[tm + r]) for r in rows]
        for r, (d0, d1) in zip(rows, slots):
            src = _row_tile(hrow_ref, i * tm + r)
            pltpu.make_async_copy(src, _row_tile(xs_ref, d0), sem).start()
            pltpu.make_async_copy(src, _row_tile(xs_ref, d1), sem).start()
        return c

    lax.fori_loop(0, tm // ISSUE_GROUP, issue, 0)

    def wait_step():
        whole = pl.ds(0, tm * ROW_SUB)
        for _ in range(2):
            pltpu.make_async_copy(hrow_ref.at[whole], xs_ref.at[whole], sem).wait()

    @pl.when(i > 0)
    def _():
        wait_step()

    @pl.when(i == pl.num_programs(0) - 1)
    def _():
        wait_step()


def _dispatch(offsets, padded, n_tiles, dest, hrow, n_slots):
    tm = TOK_TILE
    gs = pltpu.PrefetchScalarGridSpec(
        num_scalar_prefetch=3,
        grid=(dest.shape[0],),
        in_specs=[pl.BlockSpec((None, None, 2 * tm), lambda i, o, p, n: (i, 0, 0), memory_space=pltpu.SMEM),
                  pl.BlockSpec(memory_space=pl.ANY)],
        out_specs=pl.BlockSpec(memory_space=pl.ANY),
        scratch_shapes=[pltpu.VMEM((SLOT_TILE * ROW_SUB, LANES), F32), pltpu.SemaphoreType.DMA,
                        pltpu.SemaphoreType.DMA],
    )
    return pl.pallas_call(
        _dispatch_kernel, grid_spec=gs,
        out_shape=jax.ShapeDtypeStruct((n_slots * ROW_SUB, LANES), F32),
        compiler_params=_params(("arbitrary",)),
        name="dispatch",
    )(offsets, padded, n_tiles, dest, hrow)


def _ffn_kernel(te_ref, nt_ref, xs_ref, wg_ref, wu_ref, wd_ref, ys_ref):
    @pl.when(pl.program_id(0) >= nt_ref[0])
    def _():
        ys_ref[...] = jnp.zeros(ys_ref.shape, F32)

    @pl.when(pl.program_id(0) < nt_ref[0])
    def _():
        x = _from_row_tiles(xs_ref, SLOT_TILE).astype(BF16)
        a = _dot(x, wg_ref[0])
        b = _dot(x, wu_ref[0])
        act = a * _sigmoid(a) * b
        _to_row_tiles(ys_ref, _dot(act.astype(BF16), wd_ref[0]))


def _ffn(tile_expert, n_tiles, xs, wg, wu, wd, max_tiles):
    rows = SLOT_TILE * ROW_SUB
    used = lambda i, te, nt: (jnp.minimum(i, nt[0] - 1), 0)
    wspec = lambda shape: pl.BlockSpec((1,) + shape, lambda i, te, nt: (te[i], 0, 0))
    gs = pltpu.PrefetchScalarGridSpec(
        num_scalar_prefetch=2,
        grid=(max_tiles,),
        in_specs=[pl.BlockSpec((rows, LANES), used),
                  wspec((D_MODEL, D_EXPERT)), wspec((D_MODEL, D_EXPERT)), wspec((D_EXPERT, D_MODEL))],
        out_specs=pl.BlockSpec((rows, LANES), lambda i, te, nt: (i, 0)),
    )
    return pl.pallas_call(
        _ffn_kernel, grid_spec=gs,
        out_shape=jax.ShapeDtypeStruct(xs.shape, F32),
        compiler_params=_params(("arbitrary",)),
        name="expert_ffn",
    )(tile_expert, n_tiles, xs, wg, wu, wd)


def _combine_kernel(dest_ref, next_dest_ref, wcol_ref, x_ref, gfin_ref, ys_ref, o_ref, buf, sem, *, final):
    tm = TOK_TILE
    i = pl.program_id(0)
    last = pl.num_programs(0) - 1

    def gather(d_ref, slot):
        def issue(g, c):
            rows = [g * ISSUE_GROUP + u for u in range(ISSUE_GROUP)]
            slots = [(d_ref[r], d_ref[tm + r]) for r in rows]
            for r, (d0, d1) in zip(rows, slots):
                pltpu.make_async_copy(_row_tile(ys_ref, d0), _row_tile(buf.at[slot, 0], r), sem.at[slot]).start()
                pltpu.make_async_copy(_row_tile(ys_ref, d1), _row_tile(buf.at[slot, 1], r), sem.at[slot]).start()
            return c

        lax.fori_loop(0, tm // ISSUE_GROUP, issue, 0)

    slot = i % 2

    @pl.when(i == 0)
    def _():
        gather(dest_ref, 0)

    @pl.when(i < last)
    def _():
        gather(next_dest_ref, 1 - slot)

    whole = pl.ds(0, tm * ROW_SUB)
    for k in range(2):
        pltpu.make_async_copy(ys_ref.at[whole], buf.at[slot, k], sem.at[slot]).wait()
    w = wcol_ref[...]
    y = (x_ref[...] + w[:, 0:1] * _from_row_tiles(buf.at[slot, 0], tm)
         + w[:, 1:2] * _from_row_tiles(buf.at[slot, 1], tm))
    if final:
        y = _rms(y, gfin_ref[...])
    o_ref[...] = y


def _combine(dest, wcol, xf, gfin, ys, final):
    tm = TOK_TILE
    n = dest.shape[0]
    gs = pl.GridSpec(
        grid=(n,),
        in_specs=[pl.BlockSpec((None, None, 2 * tm), lambda i: (i, 0, 0), memory_space=pltpu.SMEM),
                  pl.BlockSpec((None, None, 2 * tm), lambda i: (jnp.minimum(i + 1, n - 1), 0, 0),
                               memory_space=pltpu.SMEM),
                  pl.BlockSpec((tm, LANES), lambda i: (i, 0)),
                  pl.BlockSpec((tm, D_MODEL), lambda i: (i, 0)),
                  pl.BlockSpec((1, D_MODEL), lambda i: (0, 0)),
                  pl.BlockSpec(memory_space=pl.ANY)],
        out_specs=pl.BlockSpec((tm, D_MODEL), lambda i: (i, 0)),
        scratch_shapes=[pltpu.VMEM((2, 2, tm * ROW_SUB, LANES), F32), pltpu.SemaphoreType.DMA((2,))],
    )
    return pl.pallas_call(
        functools.partial(_combine_kernel, final=final), grid_spec=gs,
        out_shape=jax.ShapeDtypeStruct(xf.shape, F32),
        compiler_params=_params(("arbitrary",)),
        name="combine",
    )(dest, dest, wcol, xf, gfin, ys)


def _alibi_slopes(n):
    return [2.0 ** (-8.0 * h / n) for h in range(1, n + 1)]


def _rope_tables(seq):
    half = C_ROPE // 2
    inv = ROPE_THETA ** (-jnp.arange(half, dtype=F32) * 2.0 / C_ROPE)
    ang = jnp.arange(seq, dtype=F32)[:, None] * inv[None, :]
    cos, sin = jnp.cos(ang), jnp.sin(ang)
    ones = jnp.ones((seq, C_NOPE), F32)
    zeros_n = jnp.zeros((seq, C_NOPE), F32)
    tail = jnp.zeros((seq, C_PAD - C_NOPE - C_ROPE), F32)
    ta = jnp.concatenate([ones, cos, cos, tail], axis=1)
    tb = jnp.concatenate([zeros_n, -sin, sin, tail], axis=1)
    return ta, tb


def _pack_in_weight(w):
    a_b_lat = w[:, 0:4224]
    kr = w[:, 4224:4256]
    gate = w[:, 4256:]
    half = C_ROPE // 2
    z = lambda n: jnp.zeros((w.shape[0], n), w.dtype)
    kr_main = jnp.concatenate([z(C_NOPE), kr, z(C_PAD - C_NOPE - C_ROPE)], axis=1)
    kr_swap = jnp.concatenate([z(C_NOPE), kr[:, half:], kr[:, :half], z(C_PAD - C_NOPE - C_ROPE)], axis=1)
    return jnp.concatenate([a_b_lat, kr_main, kr_swap, gate], axis=1).astype(BF16)


def _pack_latent_weights(w_uq, w_ukv):
    dq = C_NOPE + C_ROPE
    half = C_ROPE // 2
    rows_q = w_uq.shape[0]
    zq = lambda n: jnp.zeros((rows_q, n), w_uq.dtype)
    main, swap = [], []
    for h in range(C_HEADS):
        blk = w_uq[:, h * dq:(h + 1) * dq]
        main += [blk, zq(C_PAD - dq)]
        swap += [zq(C_NOPE), blk[:, C_NOPE + half:], blk[:, C_NOPE:C_NOPE + half], zq(C_PAD - dq)]
    wq2 = jnp.concatenate(main + swap, axis=1).astype(BF16)
    rows_k = w_ukv.shape[0]
    zk = jnp.zeros((rows_k, C_PAD - C_NOPE), w_ukv.dtype)
    dkv = C_NOPE + C_V
    kparts, vparts = [], []
    for h in range(C_HEADS):
        blk = w_ukv[:, h * dkv:(h + 1) * dkv]
        kparts += [blk[:, :C_NOPE], zk]
        vparts.append(blk[:, C_NOPE:])
    wkv2 = jnp.concatenate(kparts + vparts, axis=1).astype(BF16)
    return wq2, wkv2


def kernel(x, w_in, g_mix, p_a, p_b, p_c, w_o, g_q, w_uq, g_kv, w_ukv, lam_q1, lam_k1, lam_q2, lam_k2,
           g_sub, g_ffn, w_router, router_bias, w_gate, w_up, w_down, g_final):
    batch, seq, _ = x.shape
    xf = x.reshape(batch * seq, D_MODEL)
    ta, tb = _rope_tables(seq)
    slopes_a = _alibi_slopes(A_HEADS)
    wr = w_router.T
    wr_hi = wr.astype(BF16)
    wr_lo = (wr - wr_hi.astype(F32)).astype(BF16)
    rbias = router_bias.reshape(N_EXPERTS, 1)
    gfin = g_final.reshape(1, D_MODEL)

    for l in range(DEPTH):
        w_pack = _pack_in_weight(w_in[l])
        wq2, wkv2 = _pack_latent_weights(w_uq[l], w_ukv[l])
        qkva, qkvb, qc, kc, vc, gates = _inproj(
            xf, g_mix[l].reshape(1, -1), w_pack, g_q[l].reshape(1, -1), g_kv[l].reshape(1, -1),
            wq2, wkv2, ta, tb, seq)

        oas, las = [], []
        for g, (_, dilation) in enumerate(DIL_PAIRS):
            o, lse = _dilated(qkva, g, dilation, slopes_a[g * A_SLOTS:(g + 1) * A_SLOTS], batch, seq)
            oas.append(o)
            las.append(lse)

        lam_init = 0.8 - 0.6 * math.exp(-0.3 * l)
        lam_rows = jnp.stack([lam_q1[l], lam_k1[l], lam_q2[l], lam_k2[l]])
        ob = _diff(qkvb, lam_rows, g_sub[l].reshape(1, -1), lam_init, batch, seq)
        oc = _latent(qc, kc, vc, batch, seq)

        xf, hrow, dest, wcol, cnt = _outproj(oas, las, ob, oc, gates, xf,
                                            p_a[l].astype(BF16), p_b[l].astype(BF16), p_c[l].astype(BF16),
                                            w_o[l].astype(BF16), g_ffn[l].reshape(1, -1), wr_hi, wr_lo, rbias)
        offsets, padded, tile_expert, n_tiles, max_tiles = _slot_plan(cnt[:, 0].astype(jnp.int32), batch * seq)
        xs = _dispatch(offsets, padded, n_tiles, dest, hrow, max_tiles * SLOT_TILE)
        ys = _ffn(tile_expert, n_tiles, xs, w_gate[l].astype(BF16), w_up[l].astype(BF16), w_down[l].astype(BF16),
                  max_tiles)
        xf = _combine(dest, wcol, xf, gfin, ys, final=(l == DEPTH - 1))
    return xf.reshape(batch, seq, D_MODEL)
```

```python
import functools
import math

import jax
import jax.numpy as jnp
import numpy as np
from jax import lax
from jax.experimental import pallas as pl
from jax.experimental.pallas import tpu as pltpu

F32 = jnp.float32
BF16 = jnp.bfloat16

D_MODEL = 1024
DEPTH = 2
DIL_PAIRS = ((128, 1), (512, 4), (2048, 16))
A_SLOTS = 4
A_HEAD_DIM = 64
A_GROUP = A_SLOTS * A_HEAD_DIM
A_QKV = 3 * A_GROUP
A_HEADS = 12
A_STEPS = 128
B_HEADS = 4
B_HEAD_DIM = 64
B_QK = 512
B_V = 512
C_HEADS = 6
C_NOPE = 64
C_ROPE = 32
C_V = 64
C_Q_LORA = 256
C_KV_LORA = 128
C_PAD = 128
ROPE_THETA = 10000.0
N_EXPERTS = 16
N_EXPERT_GROUPS = 4
EXPERTS_PER_GROUP = 4
D_EXPERT = 512
RMS_EPS = 1e-6
NEG_INF = -1e30

LANES = 128
LOG2E = 1.4426950408889634
VMEM_LIMIT = 56 * 1024 * 1024

COL_A = 0
COL_B = COL_A + 3 * A_QKV
COL_CQ = COL_B + 2 * B_QK + B_V
COL_CKV = COL_CQ + C_Q_LORA
COL_KRM = COL_CKV + C_KV_LORA
COL_KRS = COL_KRM + LANES
COL_GATE = COL_KRS + LANES
COL_END = COL_GATE + 3 * D_MODEL


def _rms(x, g):
    return x * lax.rsqrt(jnp.mean(x * x, axis=-1, keepdims=True) + RMS_EPS) * g


def _sigmoid(z):
    return 1.0 / (1.0 + jnp.exp(-z))


def _dot(a, b):
    return jnp.dot(a, b, preferred_element_type=F32)


def _dot_nt(a, b):
    return lax.dot_general(a, b, (((1,), (1,)), ((), ())), preferred_element_type=F32)


def _params(sem):
    return pltpu.CompilerParams(dimension_semantics=sem, vmem_limit_bytes=VMEM_LIMIT)


def _const_spec(shape):
    nd = len(shape)
    return pl.BlockSpec(shape, lambda *_: (0,) * nd, pipeline_mode=pl.Buffered(1))


def _inproj_kernel(x_ref, gmix_ref, w_ref, gq_ref, gkv_ref, wq2_ref, wkv2_ref, ta_ref, tb_ref,
                   oa_ref, ob_ref, qc_ref, kc_ref, vc_ref, og_ref):
    u = _rms(x_ref[...], gmix_ref[...]).astype(BF16)

    def mm(c0, c1):
        return _dot(u, w_ref[:, c0:c1])

    for comp in range(3):
        y = mm(COL_A + comp * A_QKV, COL_A + (comp + 1) * A_QKV)
        if comp == 0:
            y = y * A_HEAD_DIM ** -0.5
        for g in range(len(DIL_PAIRS)):
            for hf in range(2):
                c0 = g * A_GROUP + hf * LANES
                oa_ref[(g * 3 + comp) * 2 + hf] = y[:, c0:c0 + LANES]

    b_scale = B_HEAD_DIM ** -0.5 * LOG2E
    ob_ref[:, 0:B_QK] = (mm(COL_B, COL_B + B_QK) * b_scale).astype(BF16)
    ob_ref[:, B_QK:2 * B_QK] = mm(COL_B + B_QK, COL_B + 2 * B_QK).astype(BF16)
    ob_ref[:, 2 * B_QK:2 * B_QK + B_V] = mm(COL_B + 2 * B_QK, COL_CQ).astype(BF16)

    lat = mm(COL_CQ, COL_GATE)
    cqn = _rms(lat[:, 0:C_Q_LORA], gq_ref[...]).astype(BF16)
    ckvn = _rms(lat[:, C_Q_LORA:C_Q_LORA + C_KV_LORA], gkv_ref[...]).astype(BF16)
    krm = lat[:, COL_KRM - COL_CQ:COL_KRS - COL_CQ]
    krs = lat[:, COL_KRS - COL_CQ:COL_GATE - COL_CQ]
    q2 = _dot(cqn, wq2_ref[...])
    kv2 = _dot(ckvn, wkv2_ref[...])
    ta = ta_ref[...]
    tb = tb_ref[...]
    krot = krm * ta + krs * tb
    c_scale = (C_NOPE + C_ROPE) ** -0.5 * LOG2E
    nq = C_HEADS * C_PAD
    for h in range(C_HEADS):
        sl = slice(h * C_PAD, (h + 1) * C_PAD)
        qm = q2[:, h * C_PAD:(h + 1) * C_PAD]
        qs = q2[:, nq + h * C_PAD:nq + (h + 1) * C_PAD]
        qc_ref[:, sl] = ((qm * ta + qs * tb) * c_scale).astype(BF16)
        kc_ref[:, sl] = (kv2[:, h * C_PAD:(h + 1) * C_PAD] + krot).astype(BF16)
    vc_ref[...] = kv2[:, nq:nq + C_HEADS * C_V].astype(BF16)

    gchunk = 768
    for j in range(3 * D_MODEL // gchunk):
        z = mm(COL_GATE + j * gchunk, COL_GATE + (j + 1) * gchunk)
        og_ref[:, j * gchunk:(j + 1) * gchunk] = _sigmoid(z).astype(BF16)


def _inproj(xf, gmix, w_pack, gq, gkv, wq2, wkv2, ta, tb, seq):
    T = xf.shape[0]
    tm = 512
    n_pos_blocks = seq // tm
    row = lambda w: pl.BlockSpec((tm, w), lambda i: (i, 0))
    tab = pl.BlockSpec((tm, LANES), lambda i: (i % n_pos_blocks, 0))
    widths = (2 * B_QK + B_V, C_HEADS * C_PAD, C_HEADS * C_PAD, C_HEADS * C_V, 3 * D_MODEL)
    a_slabs = 3 * A_QKV // LANES
    return pl.pallas_call(
        _inproj_kernel,
        grid=(T // tm,),
        in_specs=[row(D_MODEL), _const_spec((1, D_MODEL)), _const_spec(w_pack.shape),
                  _const_spec((1, C_Q_LORA)), _const_spec((1, C_KV_LORA)),
                  _const_spec(wq2.shape), _const_spec(wkv2.shape), tab, tab],
        out_specs=[pl.BlockSpec((a_slabs, tm, LANES), lambda i: (0, i, 0))] + [row(w) for w in widths],
        out_shape=[jax.ShapeDtypeStruct((a_slabs, T, LANES), F32)]
        + [jax.ShapeDtypeStruct((T, w), BF16) for w in widths],
        compiler_params=_params(("parallel",)),
        name="inproj",
    )(xf, gmix, w_pack, gq, gkv, wq2, wkv2, ta, tb)


def _dilated_kernel(a_ref, o_ref, l_ref, *, dilation, sub_len, slopes):
    d = dilation
    nb = sub_len // A_STEPS
    win = min(2 * A_STEPS, sub_len)
    stack = A_SLOTS * A_STEPS
    head_lane = lax.broadcasted_iota(jnp.int32, (1, A_GROUP), 1) // A_HEAD_DIM
    row = lax.broadcasted_iota(jnp.int32, (stack, 1), 0)
    slope_col = jnp.zeros((stack, 1), F32)
    for h in range(A_SLOTS):
        slope_col = jnp.where(row // A_STEPS == h, slopes[h] * d, slope_col)
    rel = row % A_STEPS - lax.broadcasted_iota(jnp.int32, (1, win), 1)

    def bias_mask(offset):
        dist = rel + offset
        return jnp.where((dist >= 0) & (dist <= A_STEPS), -slope_col * dist.astype(F32), NEG_INF)

    bm_first = bias_mask(0)
    bm_rest = bias_mask(A_STEPS) if nb > 1 else None

    def rows(first, count):
        return pl.ds(first, count) if d == 1 else pl.ds(first, count, stride=d)

    def load(pair, rws):
        return jnp.concatenate([a_ref[2 * pair, rws, :], a_ref[2 * pair + 1, rws, :]], axis=1).astype(BF16)

    for c in range(d):
        for n in range(nb):
            start = min(max((n - 1) * A_STEPS, 0), sub_len - win)
            bm = bm_first if n * A_STEPS == start else bm_rest
            q_rows = rows(c + n * A_STEPS * d, A_STEPS)
            k_rows = rows(c + start * d, win)
            q = load(0, q_rows)
            k = load(1, k_rows)
            v = load(2, k_rows)
            zero = jnp.zeros_like(q)
            q4 = jnp.concatenate([jnp.where(head_lane == h, q, zero) for h in range(A_SLOTS)], axis=0)
            s = _dot_nt(q4, k) + bm
            m = jnp.max(s, axis=-1, keepdims=True)
            p = jnp.exp(s - m)
            l = jnp.sum(p, axis=-1, keepdims=True)
            o_st = _dot(p.astype(BF16), v) / l
            lse_st = m + jnp.log(l)
            out = jnp.zeros((A_STEPS, A_GROUP), F32)
            lse = jnp.zeros((A_STEPS, A_GROUP), F32)
            for h in range(A_SLOTS):
                own = head_lane == h
                out = jnp.where(own, o_st[h * A_STEPS:(h + 1) * A_STEPS], out)
                lse = jnp.where(own, lse_st[h * A_STEPS:(h + 1) * A_STEPS], lse)
            for hf in range(2):
                o_ref[hf, q_rows, :] = out[:, hf * LANES:(hf + 1) * LANES]
                l_ref[hf, q_rows, :] = lse[:, hf * LANES:(hf + 1) * LANES]


def _dilated(a_slabs, group, dilation, slopes, batch, seq):
    kern = functools.partial(_dilated_kernel, dilation=dilation, sub_len=seq // dilation, slopes=slopes)
    o_spec = pl.BlockSpec((2, seq, LANES), lambda b: (0, b, 0))
    shape = jax.ShapeDtypeStruct((2, batch * seq, LANES), F32)
    return pl.pallas_call(
        kern,
        grid=(batch,),
        in_specs=[pl.BlockSpec((6, seq, LANES), lambda b: (group, b, 0))],
        out_specs=[o_spec, o_spec],
        out_shape=[shape, shape],
        compiler_params=_params(("parallel",)),
        name=f"dilated{group}",
    )(a_slabs)


ATT_TQ = 256
ATT_TK = 512


def _attn_kernel(*refs, diff, seq, lam_init):
    if diff:
        lam_ref, gsub_ref = refs[:2]
        refs = refs[2:]
    q_ref, k_ref, v_ref, o_ref, s_a, s_b = refs
    tq, tk = ATT_TQ, ATT_TK
    lane = lax.broadcasted_iota(jnp.int32, (1, LANES), 1)
    tri = lax.broadcasted_iota(jnp.int32, (tq, tq), 0) >= lax.broadcasted_iota(jnp.int32, (tq, tq), 1)
    if diff:
        h = pl.program_id(1)
        slope = jnp.exp2(jnp.full((1, 1), -2.0, F32) * (h + 1).astype(F32)) * LOG2E
        lam_rows = lam_ref[...]
        lam = (jnp.exp(jnp.sum(lam_rows[0:1] * lam_rows[1:2], axis=-1, keepdims=True))
               - jnp.exp(jnp.sum(lam_rows[2:3] * lam_rows[3:4], axis=-1, keepdims=True)) + lam_init)
    s_refs = (s_a, s_b)

    for qi in range(seq // tq):
        q0 = qi * tq
        q = q_ref[0, q0:q0 + tq, :]
        if diff:
            zero = jnp.zeros_like(q)
            qs = (jnp.where(lane < B_HEAD_DIM, q, zero), jnp.where(lane >= B_HEAD_DIM, q, zero))
        else:
            qs = (q[:, :C_PAD], q[:, C_PAD:])
        chunks = [(c * tk, tk, False) for c in range(q0 // tk)]
        if q0 % tk:
            chunks.append((q0 - q0 % tk, q0 % tk, False))
        chunks.append((q0, tq, True))

        m = [jnp.full((tq, LANES), NEG_INF, F32)] * 2
        for k0, w, masked in chunks:
            k = k_ref[0, k0:k0 + w, :]
            for idx in range(2):
                kk = k if diff else k[:, idx * C_PAD:(idx + 1) * C_PAD]
                s = _dot_nt(qs[idx], kk)
                if diff:
                    kpos = k0 - q0 + lax.broadcasted_iota(jnp.int32, (1, w), 1)
                    s = s + slope * kpos.astype(F32)
                if masked:
                    s = jnp.where(tri, s, NEG_INF)
                s_refs[idx][:, k0:k0 + w] = s
                for g in range(w // LANES):
                    m[idx] = jnp.maximum(m[idx], s[:, g * LANES:(g + 1) * LANES])
        row_max = [jnp.broadcast_to(jnp.max(mm, axis=-1, keepdims=True), (tq, LANES)) for mm in m]

        lsum = [jnp.zeros((tq, LANES), F32)] * 2
        acc = [jnp.zeros((tq, v_ref.shape[-1]), F32)] * 2
        for k0, w, _ in chunks:
            v = v_ref[0, k0:k0 + w, :]
            for idx in range(2):
                parts = []
                for g in range(w // LANES):
                    p = jnp.exp2(s_refs[idx][:, k0 + g * LANES:k0 + (g + 1) * LANES] - row_max[idx])
                    lsum[idx] = lsum[idx] + p
                    parts.append(p.astype(BF16))
                acc[idx] = acc[idx] + _dot(jnp.concatenate(parts, axis=1), v)
        oa = acc[0] / jnp.sum(lsum[0], axis=-1, keepdims=True)
        ob = acc[1] / jnp.sum(lsum[1], axis=-1, keepdims=True)
        if diff:
            out = _rms(oa - lam * ob, gsub_ref[...]) * (1.0 - lam_init)
        else:
            out = jnp.where(lane < C_V, oa, ob)
        o_ref[0, q0:q0 + tq, :] = out.astype(BF16)


def _attn_scratch(seq):
    return [pltpu.VMEM((ATT_TQ, seq), F32)] * 2


def _diff(qkvb, lam_rows, gsub, lam_init, batch, seq):
    x = qkvb.reshape(batch, seq, 2 * B_QK + B_V)
    kern = functools.partial(_attn_kernel, diff=True, seq=seq, lam_init=lam_init)
    out = pl.pallas_call(
        kern,
        grid=(batch, B_HEADS),
        in_specs=[_const_spec(lam_rows.shape), _const_spec(gsub.shape),
                  pl.BlockSpec((1, seq, LANES), lambda b, h: (b, 0, h)),
                  pl.BlockSpec((1, seq, LANES), lambda b, h: (b, 0, B_HEADS + h)),
                  pl.BlockSpec((1, seq, LANES), lambda b, h: (b, 0, 2 * B_HEADS + h))],
        out_specs=pl.BlockSpec((1, seq, LANES), lambda b, h: (b, 0, h)),
        out_shape=jax.ShapeDtypeStruct((batch, seq, B_V), BF16),
        scratch_shapes=_attn_scratch(seq),
        compiler_params=_params(("parallel", "parallel")),
        name="diff_attn",
    )(lam_rows, gsub, x, x, x)
    return out.reshape(batch * seq, B_V)


def _latent(qc, kc, vc, batch, seq):
    kern = functools.partial(_attn_kernel, diff=False, seq=seq, lam_init=0.0)
    out = pl.pallas_call(
        kern,
        grid=(batch, C_HEADS // 2),
        in_specs=[pl.BlockSpec((1, seq, 2 * C_PAD), lambda b, h: (b, 0, h)),
                  pl.BlockSpec((1, seq, 2 * C_PAD), lambda b, h: (b, 0, h)),
                  pl.BlockSpec((1, seq, 2 * C_V), lambda b, h: (b, 0, h))],
        out_specs=pl.BlockSpec((1, seq, 2 * C_V), lambda b, h: (b, 0, h)),
        out_shape=jax.ShapeDtypeStruct((batch, seq, C_HEADS * C_V), BF16),
        scratch_shapes=_attn_scratch(seq),
        compiler_params=_params(("parallel", "parallel")),
        name="latent_attn",
    )(qc.reshape(batch, seq, -1), kc.reshape(batch, seq, -1), vc.reshape(batch, seq, -1))
    return out.reshape(batch * seq, C_HEADS * C_V)


TOK_TILE = 512
SLOT_TILE = 256
ROW_SUB = 8
ISSUE_GROUP = 8


def _route(sc, bs):
    gscore = []
    for g in range(N_EXPERT_GROUPS):
        a, b, c, d = bs[4 * g:4 * g + 4]
        hi1, lo1 = jnp.maximum(a, b), jnp.minimum(a, b)
        hi2, lo2 = jnp.maximum(c, d), jnp.minimum(c, d)
        gscore.append(jnp.maximum(hi1, hi2) + jnp.maximum(jnp.minimum(hi1, hi2), jnp.maximum(lo1, lo2)))
    picked = []
    for g in range(N_EXPERT_GROUPS):
        ok = None
        for i in range(N_EXPERT_GROUPS):
            if i == g:
                continue
            c = gscore[g] > gscore[i] if i < g else gscore[g] >= gscore[i]
            ok = c if ok is None else ok & c
        picked.append(ok)
    sel = []
    for g in range(N_EXPERT_GROUPS):
        for j in range(EXPERTS_PER_GROUP):
            vj = bs[4 * g + j]
            rank = jnp.zeros(vj.shape, jnp.int32)
            for i in range(EXPERTS_PER_GROUP):
                if i == j:
                    continue
                vi = bs[4 * g + i]
                ahead = vi >= vj if i < j else vi > vj
                rank = rank + ahead.astype(jnp.int32)
            sel.append(picked[g] & (rank < 2))
    picked_w = [jnp.where(sel[e], sc[e], 0.0) for e in range(N_EXPERTS)]
    total = picked_w[0]
    for e in range(1, N_EXPERTS):
        total = total + picked_w[e]
    return sel, [w / total for w in picked_w]


def _outproj_kernel(oa0, la0, oa1, la1, oa2, la2, ob_ref, oc_ref, gt_ref, x_ref,
                    pa_ref, pb_ref, pc_ref, wo_ref, gffn_ref, wrh_ref, wrl_ref, rb_ref,
                    xn_ref, hrow_ref, dest_ref, wcol_ref, cnt_ref, base_ref, route_ref):
    tm = TOK_TILE
    step = pl.program_id(0)

    @pl.when(step == 0)
    def _():
        base_ref[...] = jnp.zeros(base_ref.shape, F32)

    wide = lambda ref: jnp.concatenate([ref[0], ref[1]], axis=1)
    l0, l1, l2 = wide(la0), wide(la1), wide(la2)
    mx = jnp.maximum(jnp.maximum(l0, l1), l2)
    g0, g1, g2 = jnp.exp(l0 - mx), jnp.exp(l1 - mx), jnp.exp(l2 - mx)
    o_a = (g0 * wide(oa0) + g1 * wide(oa1) + g2 * wide(oa2)) / (g0 + g1 + g2)
    ya = _dot(o_a.astype(BF16), pa_ref[...])
    yb = _dot(ob_ref[...], pb_ref[...])
    yc = _dot(oc_ref[...], pc_ref[...])
    merged = (gt_ref[:, 0:D_MODEL].astype(F32) * ya
              + gt_ref[:, D_MODEL:2 * D_MODEL].astype(F32) * yb
              + gt_ref[:, 2 * D_MODEL:3 * D_MODEL].astype(F32) * yc)
    xn = x_ref[...] + _dot(merged.astype(BF16), wo_ref[...])
    xn_ref[...] = xn
    hn = _rms(xn, gffn_ref[...])
    h_hi = hn.astype(BF16)
    h_lo = (hn - h_hi.astype(F32)).astype(BF16)
    _to_row_tiles(hrow_ref, hn)

    logits = _dot_nt(wrh_ref[...], h_hi) + _dot_nt(wrh_ref[...], h_lo) + _dot_nt(wrl_ref[...], h_hi)
    scores = _sigmoid(logits)
    biased = scores + rb_ref[...]
    sc = [scores[e:e + 1, :] for e in range(N_EXPERTS)]
    bs = [biased[e:e + 1, :] for e in range(N_EXPERTS)]
    sel, w = _route(sc, bs)
    e0 = jnp.full((1, tm), N_EXPERTS, jnp.int32)
    e1 = jnp.full((1, tm), -1, jnp.int32)
    for e in range(N_EXPERTS):
        e0 = jnp.minimum(e0, jnp.where(sel[e], e, N_EXPERTS))
        e1 = jnp.maximum(e1, jnp.where(sel[e], e, -1))
    w0 = jnp.zeros((1, tm), F32)
    w1 = jnp.zeros((1, tm), F32)
    sub = lax.broadcasted_iota(jnp.int32, (N_EXPERTS, 1), 0)
    sel_m = jnp.zeros((N_EXPERTS, tm), F32)
    for e in range(N_EXPERTS):
        w0 = jnp.where(e0 == e, w[e], w0)
        w1 = jnp.where(e1 == e, w[e], w1)
        sel_m = jnp.where((sub == e) & sel[e], 1.0, sel_m)
    upper = lax.broadcasted_iota(jnp.int32, (tm, tm), 0) < lax.broadcasted_iota(jnp.int32, (tm, tm), 1)
    prefix = _dot(sel_m.astype(BF16), upper.astype(BF16)) + base_ref[:, 0:1]
    r0 = jnp.zeros((1, tm), F32)
    r1 = jnp.zeros((1, tm), F32)
    for e in range(N_EXPERTS):
        r0 = jnp.where(e0 == e, prefix[e:e + 1, :], r0)
        r1 = jnp.where(e1 == e, prefix[e:e + 1, :], r1)
    base_ref[...] = base_ref[...] + jnp.sum(sel_m, axis=1, keepdims=True)
    cnt_ref[...] = base_ref[...]
    zrow = jnp.zeros((1, tm), jnp.int32)
    route_ref[step] = jnp.concatenate(
        [e0, e1, r0.astype(jnp.int32), r1.astype(jnp.int32), zrow, zrow, zrow, zrow], axis=0)

    @pl.when(step == pl.num_programs(0) - 1)
    def _():
        counts = base_ref[:, 0:1]
        padded = jnp.ceil(counts * (1.0 / SLOT_TILE)) * SLOT_TILE
        routes = route_ref[...]
        ea, eb = routes[:, 0, :], routes[:, 1, :]
        da, db = routes[:, 2, :], routes[:, 3, :]
        start = jnp.zeros((1, 1), F32)
        for e in range(N_EXPERTS):
            s_e = start.astype(jnp.int32)
            da = da + jnp.where(ea == e, s_e, 0)
            db = db + jnp.where(eb == e, s_e, 0)
            start = start + padded[e:e + 1, :]
        dest_ref[:, 0, :] = jnp.concatenate([da, db], axis=1)

    wcol_ref[...] = jnp.concatenate([w0, w1, jnp.zeros((LANES - 2, tm), F32)], axis=0).T


def _outproj(oas, las, ob, oc, gates, xf, pa, pb, pc, wo, gffn, wr_hi, wr_lo, rbias):
    T = xf.shape[0]
    tm = TOK_TILE
    row = lambda w: pl.BlockSpec((tm, w), lambda i: (i, 0))
    ins, specs = [], []
    for o, l in zip(oas, las):
        ins += [o, l]
        specs += [pl.BlockSpec((2, tm, LANES), lambda i: (0, i, 0))] * 2
    ins += [ob, oc, gates, xf, pa, pb, pc, wo, gffn, wr_hi, wr_lo, rbias]
    specs += [row(B_V), row(C_HEADS * C_V), row(3 * D_MODEL), row(D_MODEL)]
    specs += [_const_spec(a.shape) for a in (pa, pb, pc, wo, gffn, wr_hi, wr_lo, rbias)]
    return pl.pallas_call(
        _outproj_kernel,
        grid=(T // tm,),
        in_specs=specs,
        out_specs=[row(D_MODEL), pl.BlockSpec((tm * ROW_SUB, LANES), lambda i: (i, 0)),
                   pl.BlockSpec((T // tm, 1, 2 * tm), lambda i: (0, 0, 0)),
                   row(LANES), pl.BlockSpec((N_EXPERTS, LANES), lambda i: (0, 0))],
        out_shape=[jax.ShapeDtypeStruct((T, D_MODEL), F32), jax.ShapeDtypeStruct((T * ROW_SUB, LANES), F32),
                   jax.ShapeDtypeStruct((T // tm, 1, 2 * tm), jnp.int32), jax.ShapeDtypeStruct((T, LANES), F32),
                   jax.ShapeDtypeStruct((N_EXPERTS, LANES), F32)],
        scratch_shapes=[pltpu.VMEM((N_EXPERTS, LANES), F32), pltpu.VMEM((T // tm, 8, tm), jnp.int32)],
        compiler_params=_params(("arbitrary",)),
        name="outproj",
    )(*ins)


def _slot_plan(counts, n_tok):
    padded = (counts + SLOT_TILE - 1) // SLOT_TILE * SLOT_TILE
    ends = jnp.cumsum(padded)
    offsets = ends - padded
    max_tiles = (2 * n_tok) // SLOT_TILE + N_EXPERTS
    n_tiles = ends[-1:] // SLOT_TILE
    tile_start = jnp.minimum(jnp.arange(max_tiles, dtype=jnp.int32) * SLOT_TILE, ends[-1] - SLOT_TILE)
    tile_expert = jnp.sum((ends[None, :] <= tile_start[:, None]).astype(jnp.int32), axis=1)
    i32 = lambda a: a.astype(jnp.int32)
    return i32(offsets), i32(padded), i32(tile_expert), i32(n_tiles), max_tiles


def _to_row_tiles(ref, x):
    n = x.shape[0]
    for j in range(ROW_SUB):
        ref[pl.ds(j, n, stride=ROW_SUB), :] = x[:, j * LANES:(j + 1) * LANES]


def _from_row_tiles(ref, n):
    return jnp.concatenate([ref[pl.ds(j, n, stride=ROW_SUB), :] for j in range(ROW_SUB)], axis=1)


def _row_tile(ref, r):
    return ref.at[pl.ds(pl.multiple_of(r * ROW_SUB, ROW_SUB), ROW_SUB)]


def _dispatch_kernel(off_ref, pad_ref, nt_ref, dest_ref, hrow_ref, xs_ref, stage, zero_ref, load_sem, row_sem, zsem):
    tm = TOK_TILE
    i = pl.program_id(0)
    n = pl.num_programs(0)
    tok_rows = tm * ROW_SUB
    tile_rows = SLOT_TILE * ROW_SUB
    max_tiles = xs_ref.shape[0] // tile_rows
    slots = stage.shape[0]

    def zero_tile(t):
        return pltpu.make_async_copy(
            zero_ref, xs_ref.at[pl.ds(pl.multiple_of(t * tile_rows, tile_rows), tile_rows)], zsem)

    def load(t, slot):
        src = hrow_ref.at[pl.ds(pl.multiple_of(t * tok_rows, tok_rows), tok_rows)]
        return pltpu.make_async_copy(src, stage.at[slot], load_sem.at[slot])

    def wait_rows(slot):
        for _ in range(2):
            pltpu.make_async_copy(stage.at[slot], xs_ref.at[pl.ds(0, tok_rows)], row_sem.at[slot]).wait()

    @pl.when(i == 0)
    def _():
        zero_ref[...] = jnp.zeros(zero_ref.shape, F32)

        def fill(t, c):
            zero_tile(t).start()
            return c

        def drain(t, c):
            zero_tile(t).wait()
            return c

        lax.fori_loop(nt_ref[0], max_tiles, fill, 0)
        lax.fori_loop(nt_ref[0], max_tiles, drain, 0)
        for e in range(N_EXPERTS):
            @pl.when(pad_ref[e] > 0)
            def _():
                zero_tile((off_ref[e] + pad_ref[e]) // SLOT_TILE - 1).start()
        for e in range(N_EXPERTS):
            @pl.when(pad_ref[e] > 0)
            def _():
                zero_tile(0).wait()
        load(0, 0).start()

    slot = i % slots
    nxt = (i + 1) % slots

    @pl.when(i >= slots - 1)
    def _():
        wait_rows(nxt)

    @pl.when(i + 1 < n)
    def _():
        load(i + 1, nxt).start()

    load(i, slot).wait()

    def issue(g, c):
        rows = [g * ISSUE_GROUP + u for u in range(ISSUE_GROUP)]
        dsts = [(dest_ref[r], dest_ref[tm + r]) for r in rows]
        for r, (d0, d1) in zip(rows, dsts):
            src = _row_tile(stage.at[slot], r)
            pltpu.make_async_copy(src, _row_tile(xs_ref, d0), row_sem.at[slot]).start()
            pltpu.make_async_copy(src, _row_tile(xs_ref, d1), row_sem.at[slot]).start()
        return c

    lax.fori_loop(0, tm // ISSUE_GROUP, issue, 0)

    @pl.when(i == n - 1)
    def _():
        for back in range(slots - 2, -1, -1):
            @pl.when(i >= back)
            def _():
                wait_rows((i - back) % slots)


def _dispatch(offsets, padded, n_tiles, dest, hrow, n_slots):
    tm = TOK_TILE
    stage_slots = 3
    gs = pltpu.PrefetchScalarGridSpec(
        num_scalar_prefetch=3,
        grid=(dest.shape[0],),
        in_specs=[pl.BlockSpec((None, None, 2 * tm), lambda i, o, p, n: (i, 0, 0), memory_space=pltpu.SMEM),
                  pl.BlockSpec(memory_space=pl.ANY)],
        out_specs=pl.BlockSpec(memory_space=pl.ANY),
        scratch_shapes=[pltpu.VMEM((stage_slots, tm * ROW_SUB, LANES), F32),
                        pltpu.VMEM((SLOT_TILE * ROW_SUB, LANES), F32),
                        pltpu.SemaphoreType.DMA((stage_slots,)), pltpu.SemaphoreType.DMA((stage_slots,)),
                        pltpu.SemaphoreType.DMA],
    )
    return pl.pallas_call(
        _dispatch_kernel, grid_spec=gs,
        out_shape=jax.ShapeDtypeStruct((n_slots * ROW_SUB, LANES), F32),
        compiler_params=_params(("arbitrary",)),
        name="dispatch",
    )(offsets, padded, n_tiles, dest, hrow)


def _ffn_kernel(te_ref, nt_ref, xs_ref, wg_ref, wu_ref, wd_ref, ys_ref, wg_b, wu_b, wd_b):
    i = pl.program_id(0)
    used = i < nt_ref[0]

    @pl.when(jnp.logical_not(used))
    def _():
        ys_ref[...] = jnp.zeros(ys_ref.shape, F32)

    @pl.when(used & ((i == 0) | (te_ref[i] != te_ref[jnp.maximum(i - 1, 0)])))
    def _():
        wg_b[...] = wg_ref[0].astype(BF16)
        wu_b[...] = wu_ref[0].astype(BF16)
        wd_b[...] = wd_ref[0].astype(BF16)

    @pl.when(used)
    def _():
        x = _from_row_tiles(xs_ref, SLOT_TILE).astype(BF16)
        a = _dot(x, wg_b[...])
        b = _dot(x, wu_b[...])
        act = a * _sigmoid(a) * b
        _to_row_tiles(ys_ref, _dot(act.astype(BF16), wd_b[...]))


def _ffn(tile_expert, n_tiles, xs, wg, wu, wd, layer, max_tiles):
    rows = SLOT_TILE * ROW_SUB
    used = lambda i, te, nt: (jnp.minimum(i, nt[0] - 1), 0)
    wspec = lambda shape: pl.BlockSpec((None, 1) + shape, lambda i, te, nt: (layer, te[i], 0, 0))
    gs = pltpu.PrefetchScalarGridSpec(
        num_scalar_prefetch=2,
        grid=(max_tiles,),
        in_specs=[pl.BlockSpec((rows, LANES), used),
                  wspec((D_MODEL, D_EXPERT)), wspec((D_MODEL, D_EXPERT)), wspec((D_EXPERT, D_MODEL))],
        out_specs=pl.BlockSpec((rows, LANES), lambda i, te, nt: (i, 0)),
        scratch_shapes=[pltpu.VMEM((D_MODEL, D_EXPERT), BF16), pltpu.VMEM((D_MODEL, D_EXPERT), BF16),
                        pltpu.VMEM((D_EXPERT, D_MODEL), BF16)],
    )
    return pl.pallas_call(
        _ffn_kernel, grid_spec=gs,
        out_shape=jax.ShapeDtypeStruct(xs.shape, F32),
        compiler_params=_params(("arbitrary",)),
        name="expert_ffn",
    )(tile_expert, n_tiles, xs, wg, wu, wd)


def _combine_kernel(dest_ref, next_dest_ref, wcol_ref, x_ref, gfin_ref, ys_ref, o_ref, buf, sem, *, final):
    tm = TOK_TILE
    i = pl.program_id(0)
    last = pl.num_programs(0) - 1

    def gather(d_ref, slot):
        def issue(g, c):
            rows = [g * ISSUE_GROUP + u for u in range(ISSUE_GROUP)]
            slots = [(d_ref[r], d_ref[tm + r]) for r in rows]
            for r, (d0, d1) in zip(rows, slots):
                pltpu.make_async_copy(_row_tile(ys_ref, d0), _row_tile(buf.at[slot, 0], r), sem.at[slot]).start()
                pltpu.make_async_copy(_row_tile(ys_ref, d1), _row_tile(buf.at[slot, 1], r), sem.at[slot]).start()
            return c

        lax.fori_loop(0, tm // ISSUE_GROUP, issue, 0)

    slot = i % 2

    @pl.when(i == 0)
    def _():
        gather(dest_ref, 0)

    @pl.when(i < last)
    def _():
        gather(next_dest_ref, 1 - slot)

    whole = pl.ds(0, tm * ROW_SUB)
    for k in range(2):
        pltpu.make_async_copy(ys_ref.at[whole], buf.at[slot, k], sem.at[slot]).wait()
    w = wcol_ref[...]
    y = (x_ref[...] + w[:, 0:1] * _from_row_tiles(buf.at[slot, 0], tm)
         + w[:, 1:2] * _from_row_tiles(buf.at[slot, 1], tm))
    if final:
        y = _rms(y, gfin_ref[...])
    o_ref[...] = y


def _combine(dest, wcol, xf, gfin, ys, final):
    tm = TOK_TILE
    n = dest.shape[0]
    gs = pl.GridSpec(
        grid=(n,),
        in_specs=[pl.BlockSpec((None, None, 2 * tm), lambda i: (i, 0, 0), memory_space=pltpu.SMEM),
                  pl.BlockSpec((None, None, 2 * tm), lambda i: (jnp.minimum(i + 1, n - 1), 0, 0),
                               memory_space=pltpu.SMEM),
                  pl.BlockSpec((tm, LANES), lambda i: (i, 0)),
                  pl.BlockSpec((tm, D_MODEL), lambda i: (i, 0)),
                  pl.BlockSpec((1, D_MODEL), lambda i: (0, 0)),
                  pl.BlockSpec(memory_space=pl.ANY)],
        out_specs=pl.BlockSpec((tm, D_MODEL), lambda i: (i, 0)),
        scratch_shapes=[pltpu.VMEM((2, 2, tm * ROW_SUB, LANES), F32), pltpu.SemaphoreType.DMA((2,))],
    )
    return pl.pallas_call(
        functools.partial(_combine_kernel, final=final), grid_spec=gs,
        out_shape=jax.ShapeDtypeStruct(xf.shape, F32),
        compiler_params=_params(("arbitrary",)),
        name="combine",
    )(dest, dest, wcol, xf, gfin, ys)


def _alibi_slopes(n):
    return [2.0 ** (-8.0 * h / n) for h in range(1, n + 1)]


def _rope_tables(seq):
    half = C_ROPE // 2
    inv = ROPE_THETA ** (-jnp.arange(half, dtype=F32) * 2.0 / C_ROPE)
    ang = jnp.arange(seq, dtype=F32)[:, None] * inv[None, :]
    cos, sin = jnp.cos(ang), jnp.sin(ang)
    ones = jnp.ones((seq, C_NOPE), F32)
    zeros_n = jnp.zeros((seq, C_NOPE), F32)
    tail = jnp.zeros((seq, C_PAD - C_NOPE - C_ROPE), F32)
    ta = jnp.concatenate([ones, cos, cos, tail], axis=1)
    tb = jnp.concatenate([zeros_n, -sin, sin, tail], axis=1)
    return ta, tb


def _pack_in_weight(w):
    w = w.astype(BF16)
    a_b_lat = w[:, 0:4224]
    kr = w[:, 4224:4256]
    gate = w[:, 4256:]
    half = C_ROPE // 2
    z = lambda n: jnp.zeros((w.shape[0], n), w.dtype)
    kr_main = jnp.concatenate([z(C_NOPE), kr, z(C_PAD - C_NOPE - C_ROPE)], axis=1)
    kr_swap = jnp.concatenate([z(C_NOPE), kr[:, half:], kr[:, :half], z(C_PAD - C_NOPE - C_ROPE)], axis=1)
    return jnp.concatenate([a_b_lat, kr_main, kr_swap, gate], axis=1)


def _pack_latent_weights(w_uq, w_ukv):
    dq = C_NOPE + C_ROPE
    half = C_ROPE // 2
    rows_q = w_uq.shape[0]
    zq = lambda n: jnp.zeros((rows_q, n), w_uq.dtype)
    main, swap = [], []
    for h in range(C_HEADS):
        blk = w_uq[:, h * dq:(h + 1) * dq]
        main += [blk, zq(C_PAD - dq)]
        swap += [zq(C_NOPE), blk[:, C_NOPE + half:], blk[:, C_NOPE:C_NOPE + half], zq(C_PAD - dq)]
    wq2 = jnp.concatenate(main + swap, axis=1).astype(BF16)
    rows_k = w_ukv.shape[0]
    zk = jnp.zeros((rows_k, C_PAD - C_NOPE), w_ukv.dtype)
    dkv = C_NOPE + C_V
    kparts, vparts = [], []
    for h in range(C_HEADS):
        blk = w_ukv[:, h * dkv:(h + 1) * dkv]
        kparts += [blk[:, :C_NOPE], zk]
        vparts.append(blk[:, C_NOPE:])
    wkv2 = jnp.concatenate(kparts + vparts, axis=1).astype(BF16)
    return wq2, wkv2


def kernel(x, w_in, g_mix, p_a, p_b, p_c, w_o, g_q, w_uq, g_kv, w_ukv, lam_q1, lam_k1, lam_q2, lam_k2,
           g_sub, g_ffn, w_router, router_bias, w_gate, w_up, w_down, g_final):
    batch, seq, _ = x.shape
    xf = x.reshape(batch * seq, D_MODEL)
    ta, tb = _rope_tables(seq)
    slopes_a = _alibi_slopes(A_HEADS)
    wr = w_router.T
    wr_hi = wr.astype(BF16)
    wr_lo = (wr - wr_hi.astype(F32)).astype(BF16)
    rbias = router_bias.reshape(N_EXPERTS, 1)
    gfin = g_final.reshape(1, D_MODEL)

    for l in range(DEPTH):
        w_pack = _pack_in_weight(w_in[l])
        wq2, wkv2 = _pack_latent_weights(w_uq[l], w_ukv[l])
        qkva, qkvb, qc, kc, vc, gates = _inproj(
            xf, g_mix[l].reshape(1, -1), w_pack, g_q[l].reshape(1, -1), g_kv[l].reshape(1, -1),
            wq2, wkv2, ta, tb, seq)

        oas, las = [], []
        for g, (_, dilation) in enumerate(DIL_PAIRS):
            o, lse = _dilated(qkva, g, dilation, slopes_a[g * A_SLOTS:(g + 1) * A_SLOTS], batch, seq)
            oas.append(o)
            las.append(lse)

        lam_init = 0.8 - 0.6 * math.exp(-0.3 * l)
        lam_rows = jnp.stack([lam_q1[l], lam_k1[l], lam_q2[l], lam_k2[l]])
        ob = _diff(qkvb, lam_rows, g_sub[l].reshape(1, -1), lam_init, batch, seq)
        oc = _latent(qc, kc, vc, batch, seq)

        xf, hrow, dest, wcol, cnt = _outproj(oas, las, ob, oc, gates, xf,
                                            p_a[l].astype(BF16), p_b[l].astype(BF16), p_c[l].astype(BF16),
                                            w_o[l].astype(BF16), g_ffn[l].reshape(1, -1), wr_hi, wr_lo, rbias)
        offsets, padded, tile_expert, n_tiles, max_tiles = _slot_plan(cnt[:, 0].astype(jnp.int32), batch * seq)
        xs = _dispatch(offsets, padded, n_tiles, dest, hrow, max_tiles * SLOT_TILE)
        ys = _ffn(tile_expert, n_tiles, xs, w_gate, w_up, w_down, l, max_tiles)
        xf = _combine(dest, wcol, xf, gfin, ys, final=(l == DEPTH - 1))
    return xf.reshape(batch, seq, D_MODEL)
```

```python
import functools
import math

import jax
import jax.numpy as jnp
import numpy as np
from jax import lax
from jax.experimental import pallas as pl
from jax.experimental.pallas import tpu as pltpu

F32 = jnp.float32
BF16 = jnp.bfloat16

D_MODEL = 1024
DEPTH = 2
DIL_PAIRS = ((128, 1), (512, 4), (2048, 16))
A_SLOTS = 4
A_HEAD_DIM = 64
A_GROUP = A_SLOTS * A_HEAD_DIM
A_QKV = 3 * A_GROUP
A_HEADS = 12
A_STEPS = 128
B_HEADS = 4
B_HEAD_DIM = 64
B_QK = 512
B_V = 512
C_HEADS = 6
C_NOPE = 64
C_ROPE = 32
C_V = 64
C_Q_LORA = 256
C_KV_LORA = 128
C_PAD = 128
ROPE_THETA = 10000.0
N_EXPERTS = 16
N_EXPERT_GROUPS = 4
EXPERTS_PER_GROUP = 4
D_EXPERT = 512
RMS_EPS = 1e-6
NEG_INF = -1e30

LANES = 128
LOG2E = 1.4426950408889634
VMEM_LIMIT = 56 * 1024 * 1024

COL_A = 0
COL_B = COL_A + 3 * A_QKV
COL_CQ = COL_B + 2 * B_QK + B_V
COL_CKV = COL_CQ + C_Q_LORA
COL_KRM = COL_CKV + C_KV_LORA
COL_KRS = COL_KRM + LANES
COL_GATE = COL_KRS + LANES
COL_END = COL_GATE + 3 * D_MODEL


def _rms(x, g):
    return x * lax.rsqrt(jnp.mean(x * x, axis=-1, keepdims=True) + RMS_EPS) * g


def _sigmoid(z):
    return 1.0 / (1.0 + jnp.exp(-z))


def _dot(a, b):
    return jnp.dot(a, b, preferred_element_type=F32)


def _dot_nt(a, b):
    return lax.dot_general(a, b, (((1,), (1,)), ((), ())), preferred_element_type=F32)


def _params(sem):
    return pltpu.CompilerParams(dimension_semantics=sem, vmem_limit_bytes=VMEM_LIMIT)


def _const_spec(shape):
    nd = len(shape)
    return pl.BlockSpec(shape, lambda *_: (0,) * nd, pipeline_mode=pl.Buffered(1))


def _inproj_kernel(x_ref, gmix_ref, w_ref, gq_ref, gkv_ref, wq2_ref, wkv2_ref, ta_ref, tb_ref,
                   oa_ref, ob_ref, qc_ref, kc_ref, vc_ref, og_ref):
    u = _rms(x_ref[...], gmix_ref[...]).astype(BF16)

    def mm(c0, c1):
        return _dot(u, w_ref[:, c0:c1])

    for comp in range(3):
        y = mm(COL_A + comp * A_QKV, COL_A + (comp + 1) * A_QKV)
        if comp == 0:
            y = y * A_HEAD_DIM ** -0.5
        for g in range(len(DIL_PAIRS)):
            for hf in range(2):
                c0 = g * A_GROUP + hf * LANES
                oa_ref[(g * 3 + comp) * 2 + hf] = y[:, c0:c0 + LANES]

    b_scale = B_HEAD_DIM ** -0.5 * LOG2E
    ob_ref[:, 0:B_QK] = (mm(COL_B, COL_B + B_QK) * b_scale).astype(BF16)
    ob_ref[:, B_QK:2 * B_QK] = mm(COL_B + B_QK, COL_B + 2 * B_QK).astype(BF16)
    ob_ref[:, 2 * B_QK:2 * B_QK + B_V] = mm(COL_B + 2 * B_QK, COL_CQ).astype(BF16)

    lat = mm(COL_CQ, COL_GATE)
    cqn = _rms(lat[:, 0:C_Q_LORA], gq_ref[...]).astype(BF16)
    ckvn = _rms(lat[:, C_Q_LORA:C_Q_LORA + C_KV_LORA], gkv_ref[...]).astype(BF16)
    krm = lat[:, COL_KRM - COL_CQ:COL_KRS - COL_CQ]
    krs = lat[:, COL_KRS - COL_CQ:COL_GATE - COL_CQ]
    q2 = _dot(cqn, wq2_ref[...])
    kv2 = _dot(ckvn, wkv2_ref[...])
    ta = ta_ref[...]
    tb = tb_ref[...]
    krot = krm * ta + krs * tb
    c_scale = (C_NOPE + C_ROPE) ** -0.5 * LOG2E
    nq = C_HEADS * C_PAD
    for h in range(C_HEADS):
        sl = slice(h * C_PAD, (h + 1) * C_PAD)
        qm = q2[:, h * C_PAD:(h + 1) * C_PAD]
        qs = q2[:, nq + h * C_PAD:nq + (h + 1) * C_PAD]
        qc_ref[:, sl] = ((qm * ta + qs * tb) * c_scale).astype(BF16)
        kc_ref[:, sl] = (kv2[:, h * C_PAD:(h + 1) * C_PAD] + krot).astype(BF16)
    vc_ref[...] = kv2[:, nq:nq + C_HEADS * C_V].astype(BF16)

    gchunk = 768
    for j in range(3 * D_MODEL // gchunk):
        z = mm(COL_GATE + j * gchunk, COL_GATE + (j + 1) * gchunk)
        og_ref[:, j * gchunk:(j + 1) * gchunk] = _sigmoid(z).astype(BF16)


def _inproj(xf, gmix, w_pack, gq, gkv, wq2, wkv2, ta, tb, seq):
    T = xf.shape[0]
    tm = 512
    n_pos_blocks = seq // tm
    row = lambda w: pl.BlockSpec((tm, w), lambda i: (i, 0))
    tab = pl.BlockSpec((tm, LANES), lambda i: (i % n_pos_blocks, 0))
    widths = (2 * B_QK + B_V, C_HEADS * C_PAD, C_HEADS * C_PAD, C_HEADS * C_V, 3 * D_MODEL)
    a_slabs = 3 * A_QKV // LANES
    return pl.pallas_call(
        _inproj_kernel,
        grid=(T // tm,),
        in_specs=[row(D_MODEL), _const_spec((1, D_MODEL)), _const_spec(w_pack.shape),
                  _const_spec((1, C_Q_LORA)), _const_spec((1, C_KV_LORA)),
                  _const_spec(wq2.shape), _const_spec(wkv2.shape), tab, tab],
        out_specs=[pl.BlockSpec((a_slabs, tm, LANES), lambda i: (0, i, 0))] + [row(w) for w in widths],
        out_shape=[jax.ShapeDtypeStruct((a_slabs, T, LANES), F32)]
        + [jax.ShapeDtypeStruct((T, w), BF16) for w in widths],
        compiler_params=_params(("parallel",)),
        name="inproj",
    )(xf, gmix, w_pack, gq, gkv, wq2, wkv2, ta, tb)


def _dilated_kernel(a_ref, o_ref, l_ref, *, dilation, sub_len, slopes):
    d = dilation
    nb = sub_len // A_STEPS
    win = min(2 * A_STEPS, sub_len)
    stack = A_SLOTS * A_STEPS
    head_lane = lax.broadcasted_iota(jnp.int32, (1, A_GROUP), 1) // A_HEAD_DIM
    row = lax.broadcasted_iota(jnp.int32, (stack, 1), 0)
    slope_col = jnp.zeros((stack, 1), F32)
    for h in range(A_SLOTS):
        slope_col = jnp.where(row // A_STEPS == h, slopes[h] * d, slope_col)
    rel = row % A_STEPS - lax.broadcasted_iota(jnp.int32, (1, win), 1)

    def bias_mask(offset):
        dist = rel + offset
        return jnp.where((dist >= 0) & (dist <= A_STEPS), -slope_col * dist.astype(F32), NEG_INF)

    bm_first = bias_mask(0)
    bm_rest = bias_mask(A_STEPS) if nb > 1 else None

    def rows(first, count):
        return pl.ds(first, count) if d == 1 else pl.ds(first, count, stride=d)

    def load(pair, rws):
        return jnp.concatenate([a_ref[2 * pair, rws, :], a_ref[2 * pair + 1, rws, :]], axis=1).astype(BF16)

    for c in range(d):
        for n in range(nb):
            start = min(max((n - 1) * A_STEPS, 0), sub_len - win)
            bm = bm_first if n * A_STEPS == start else bm_rest
            q_rows = rows(c + n * A_STEPS * d, A_STEPS)
            k_rows = rows(c + start * d, win)
            q = load(0, q_rows)
            k = load(1, k_rows)
            v = load(2, k_rows)
            zero = jnp.zeros_like(q)
            q4 = jnp.concatenate([jnp.where(head_lane == h, q, zero) for h in range(A_SLOTS)], axis=0)
            s = _dot_nt(q4, k) + bm
            m = jnp.max(s, axis=-1, keepdims=True)
            p = jnp.exp(s - m)
            l = jnp.sum(p, axis=-1, keepdims=True)
            o_st = _dot(p.astype(BF16), v) / l
            lse_st = m + jnp.log(l)
            out = jnp.zeros((A_STEPS, A_GROUP), F32)
            lse = jnp.zeros((A_STEPS, A_GROUP), F32)
            for h in range(A_SLOTS):
                own = head_lane == h
                out = jnp.where(own, o_st[h * A_STEPS:(h + 1) * A_STEPS], out)
                lse = jnp.where(own, lse_st[h * A_STEPS:(h + 1) * A_STEPS], lse)
            for hf in range(2):
                o_ref[hf, q_rows, :] = out[:, hf * LANES:(hf + 1) * LANES]
                l_ref[hf, q_rows, :] = lse[:, hf * LANES:(hf + 1) * LANES]


def _dilated(a_slabs, group, dilation, slopes, batch, seq):
    kern = functools.partial(_dilated_kernel, dilation=dilation, sub_len=seq // dilation, slopes=slopes)
    o_spec = pl.BlockSpec((2, seq, LANES), lambda b: (0, b, 0))
    shape = jax.ShapeDtypeStruct((2, batch * seq, LANES), F32)
    return pl.pallas_call(
        kern,
        grid=(batch,),
        in_specs=[pl.BlockSpec((6, seq, LANES), lambda b: (group, b, 0))],
        out_specs=[o_spec, o_spec],
        out_shape=[shape, shape],
        compiler_params=_params(("parallel",)),
        name=f"dilated{group}",
    )(a_slabs)


ATT_TQ = 512
ATT_TK = 512
ATT_SUB = 256


def _tree(op, xs):
    while len(xs) > 1:
        xs = [op(xs[i], xs[i + 1]) if i + 1 < len(xs) else xs[i] for i in range(0, len(xs), 2)]
    return xs[0]


def _attn_kernel(*refs, diff, seq, lam_init):
    if diff:
        lam_ref, gsub_ref = refs[:2]
        refs = refs[2:]
    q_ref, k_ref, v_ref, o_ref, s_a, s_b = refs
    tq, tk, sub = ATT_TQ, ATT_TK, ATT_SUB
    n_sub = tq // sub
    lane = lax.broadcasted_iota(jnp.int32, (1, LANES), 1)
    tri = lax.broadcasted_iota(jnp.int32, (sub, sub), 0) >= lax.broadcasted_iota(jnp.int32, (sub, sub), 1)
    if diff:
        h = pl.program_id(1)
        slope = jnp.exp2(jnp.full((1, 1), -2.0, F32) * (h + 1).astype(F32)) * LOG2E
        lam_rows = lam_ref[...]
        lam = (jnp.exp(jnp.sum(lam_rows[0:1] * lam_rows[1:2], axis=-1, keepdims=True))
               - jnp.exp(jnp.sum(lam_rows[2:3] * lam_rows[3:4], axis=-1, keepdims=True)) + lam_init)
    s_refs = (s_a, s_b)

    for qi in range(seq // tq):
        q0 = qi * tq
        q = q_ref[0, q0:q0 + tq, :]
        if diff:
            zero = jnp.zeros_like(q)
            qs = (jnp.where(lane < B_HEAD_DIM, q, zero), jnp.where(lane >= B_HEAD_DIM, q, zero))
        else:
            qs = (q[:, :C_PAD], q[:, C_PAD:])
        work = [(0, tq, c * tk, tk, False) for c in range(q0 // tk)]
        for i in range(n_sub):
            for j in range(i + 1):
                work.append((i * sub, sub, q0 + j * sub, sub, i == j))

        m = [[jnp.full((sub, LANES), NEG_INF, F32)] * n_sub for _ in range(2)]
        for r0, nr, k0, w, masked in work:
            k = k_ref[0, k0:k0 + w, :]
            for idx in range(2):
                kk = k if diff else k[:, idx * C_PAD:(idx + 1) * C_PAD]
                s = _dot_nt(qs[idx][r0:r0 + nr], kk)
                if diff:
                    kpos = k0 - q0 + lax.broadcasted_iota(jnp.int32, (1, w), 1)
                    s = s + slope * kpos.astype(F32)
                if masked:
                    s = jnp.where(tri, s, NEG_INF)
                s_refs[idx][r0:r0 + nr, k0:k0 + w] = s
                for part in range(nr // sub):
                    blk = r0 // sub + part
                    cols = [s[part * sub:(part + 1) * sub, g * LANES:(g + 1) * LANES] for g in range(w // LANES)]
                    m[idx][blk] = jnp.maximum(m[idx][blk], _tree(jnp.maximum, cols))
        row_max = [[jnp.broadcast_to(jnp.max(mm, axis=-1, keepdims=True), (sub, LANES)) for mm in m[idx]]
                   for idx in range(2)]

        dv = v_ref.shape[-1]
        lsum = [[jnp.zeros((sub, LANES), F32)] * n_sub for _ in range(2)]
        acc_tile = [jnp.zeros((tq, dv), F32)] * 2
        acc_blk = [[jnp.zeros((sub, dv), F32)] * n_sub for _ in range(2)]
        for r0, nr, k0, w, _ in work:
            v = v_ref[0, k0:k0 + w, :]
            for idx in range(2):
                rows = []
                for part in range(nr // sub):
                    blk = r0 // sub + part
                    lo = r0 + part * sub
                    ps = [jnp.exp2(s_refs[idx][lo:lo + sub, k0 + g * LANES:k0 + (g + 1) * LANES] - row_max[idx][blk])
                          for g in range(w // LANES)]
                    lsum[idx][blk] = lsum[idx][blk] + _tree(jnp.add, ps)
                    rows.append(jnp.concatenate([p.astype(BF16) for p in ps], axis=1))
                if nr == tq:
                    acc_tile[idx] = acc_tile[idx] + _dot(jnp.concatenate(rows, axis=0), v)
                else:
                    acc_blk[idx][r0 // sub] = acc_blk[idx][r0 // sub] + _dot(rows[0], v)
        for blk in range(n_sub):
            rows_blk = slice(blk * sub, (blk + 1) * sub)
            oa = (acc_tile[0][rows_blk] + acc_blk[0][blk]) / jnp.sum(lsum[0][blk], axis=-1, keepdims=True)
            ob = (acc_tile[1][rows_blk] + acc_blk[1][blk]) / jnp.sum(lsum[1][blk], axis=-1, keepdims=True)
            if diff:
                out = _rms(oa - lam * ob, gsub_ref[...]) * (1.0 - lam_init)
            else:
                out = jnp.where(lane < C_V, oa, ob)
            o_ref[0, q0 + blk * sub:q0 + (blk + 1) * sub, :] = out.astype(BF16)


def _attn_scratch(seq):
    return [pltpu.VMEM((ATT_TQ, seq), F32)] * 2


def _diff(qkvb, lam_rows, gsub, lam_init, batch, seq):
    x = qkvb.reshape(batch, seq, 2 * B_QK + B_V)
    kern = functools.partial(_attn_kernel, diff=True, seq=seq, lam_init=lam_init)
    out = pl.pallas_call(
        kern,
        grid=(batch, B_HEADS),
        in_specs=[_const_spec(lam_rows.shape), _const_spec(gsub.shape),
                  pl.BlockSpec((1, seq, LANES), lambda b, h: (b, 0, h)),
                  pl.BlockSpec((1, seq, LANES), lambda b, h: (b, 0, B_HEADS + h)),
                  pl.BlockSpec((1, seq, LANES), lambda b, h: (b, 0, 2 * B_HEADS + h))],
        out_specs=pl.BlockSpec((1, seq, LANES), lambda b, h: (b, 0, h)),
        out_shape=jax.ShapeDtypeStruct((batch, seq, B_V), BF16),
        scratch_shapes=_attn_scratch(seq),
        compiler_params=_params(("parallel", "parallel")),
        name="diff_attn",
    )(lam_rows, gsub, x, x, x)
    return out.reshape(batch * seq, B_V)


def _latent(qc, kc, vc, batch, seq):
    kern = functools.partial(_attn_kernel, diff=False, seq=seq, lam_init=0.0)
    out = pl.pallas_call(
        kern,
        grid=(batch, C_HEADS // 2),
        in_specs=[pl.BlockSpec((1, seq, 2 * C_PAD), lambda b, h: (b, 0, h)),
                  pl.BlockSpec((1, seq, 2 * C_PAD), lambda b, h: (b, 0, h)),
                  pl.BlockSpec((1, seq, 2 * C_V), lambda b, h: (b, 0, h))],
        out_specs=pl.BlockSpec((1, seq, 2 * C_V), lambda b, h: (b, 0, h)),
        out_shape=jax.ShapeDtypeStruct((batch, seq, C_HEADS * C_V), BF16),
        scratch_shapes=_attn_scratch(seq),
        compiler_params=_params(("parallel", "parallel")),
        name="latent_attn",
    )(qc.reshape(batch, seq, -1), kc.reshape(batch, seq, -1), vc.reshape(batch, seq, -1))
    return out.reshape(batch * seq, C_HEADS * C_V)


TOK_TILE = 512
SLOT_TILE = 512
ROW_SUB = 8
ISSUE_GROUP = 8


def _route(sc, bs):
    gscore = []
    for g in range(N_EXPERT_GROUPS):
        a, b, c, d = bs[4 * g:4 * g + 4]
        hi1, lo1 = jnp.maximum(a, b), jnp.minimum(a, b)
        hi2, lo2 = jnp.maximum(c, d), jnp.minimum(c, d)
        gscore.append(jnp.maximum(hi1, hi2) + jnp.maximum(jnp.minimum(hi1, hi2), jnp.maximum(lo1, lo2)))
    picked = []
    for g in range(N_EXPERT_GROUPS):
        ok = None
        for i in range(N_EXPERT_GROUPS):
            if i == g:
                continue
            c = gscore[g] > gscore[i] if i < g else gscore[g] >= gscore[i]
            ok = c if ok is None else ok & c
        picked.append(ok)
    sel = []
    for g in range(N_EXPERT_GROUPS):
        for j in range(EXPERTS_PER_GROUP):
            vj = bs[4 * g + j]
            rank = jnp.zeros(vj.shape, jnp.int32)
            for i in range(EXPERTS_PER_GROUP):
                if i == j:
                    continue
                vi = bs[4 * g + i]
                ahead = vi >= vj if i < j else vi > vj
                rank = rank + ahead.astype(jnp.int32)
            sel.append(picked[g] & (rank < 2))
    picked_w = [jnp.where(sel[e], sc[e], 0.0) for e in range(N_EXPERTS)]
    total = picked_w[0]
    for e in range(1, N_EXPERTS):
        total = total + picked_w[e]
    return sel, [w / total for w in picked_w]


def _outproj_kernel(oa0, la0, oa1, la1, oa2, la2, ob_ref, oc_ref, gt_ref, x_ref,
                    pa_ref, pb_ref, pc_ref, wo_ref, gffn_ref, wrh_ref, wrl_ref, rb_ref,
                    xn_ref, hrow_ref, dest_ref, wcol_ref, cnt_ref, base_ref, route_ref):
    tm = TOK_TILE
    step = pl.program_id(0)

    @pl.when(step == 0)
    def _():
        base_ref[...] = jnp.zeros(base_ref.shape, F32)

    wide = lambda ref: jnp.concatenate([ref[0], ref[1]], axis=1)
    l0, l1, l2 = wide(la0), wide(la1), wide(la2)
    mx = jnp.maximum(jnp.maximum(l0, l1), l2)
    g0, g1, g2 = jnp.exp(l0 - mx), jnp.exp(l1 - mx), jnp.exp(l2 - mx)
    o_a = (g0 * wide(oa0) + g1 * wide(oa1) + g2 * wide(oa2)) / (g0 + g1 + g2)
    ya = _dot(o_a.astype(BF16), pa_ref[...])
    yb = _dot(ob_ref[...], pb_ref[...])
    yc = _dot(oc_ref[...], pc_ref[...])
    merged = (gt_ref[:, 0:D_MODEL].astype(F32) * ya
              + gt_ref[:, D_MODEL:2 * D_MODEL].astype(F32) * yb
              + gt_ref[:, 2 * D_MODEL:3 * D_MODEL].astype(F32) * yc)
    xn = x_ref[...] + _dot(merged.astype(BF16), wo_ref[...])
    xn_ref[...] = xn
    hn = _rms(xn, gffn_ref[...])
    h_hi = hn.astype(BF16)
    h_lo = (hn - h_hi.astype(F32)).astype(BF16)
    _to_row_tiles(hrow_ref, hn)

    logits = _dot_nt(wrh_ref[...], h_hi) + _dot_nt(wrh_ref[...], h_lo) + _dot_nt(wrl_ref[...], h_hi)
    scores = _sigmoid(logits)
    biased = scores + rb_ref[...]
    sc = [scores[e:e + 1, :] for e in range(N_EXPERTS)]
    bs = [biased[e:e + 1, :] for e in range(N_EXPERTS)]
    sel, w = _route(sc, bs)
    e0 = jnp.full((1, tm), N_EXPERTS, jnp.int32)
    e1 = jnp.full((1, tm), -1, jnp.int32)
    for e in range(N_EXPERTS):
        e0 = jnp.minimum(e0, jnp.where(sel[e], e, N_EXPERTS))
        e1 = jnp.maximum(e1, jnp.where(sel[e], e, -1))
    w0 = jnp.zeros((1, tm), F32)
    w1 = jnp.zeros((1, tm), F32)
    sub = lax.broadcasted_iota(jnp.int32, (N_EXPERTS, 1), 0)
    sel_m = jnp.zeros((N_EXPERTS, tm), F32)
    for e in range(N_EXPERTS):
        w0 = jnp.where(e0 == e, w[e], w0)
        w1 = jnp.where(e1 == e, w[e], w1)
        sel_m = jnp.where((sub == e) & sel[e], 1.0, sel_m)
    upper = lax.broadcasted_iota(jnp.int32, (tm, tm), 0) < lax.broadcasted_iota(jnp.int32, (tm, tm), 1)
    prefix = _dot(sel_m.astype(BF16), upper.astype(BF16)) + base_ref[:, 0:1]
    r0 = jnp.zeros((1, tm), F32)
    r1 = jnp.zeros((1, tm), F32)
    for e in range(N_EXPERTS):
        r0 = jnp.where(e0 == e, prefix[e:e + 1, :], r0)
        r1 = jnp.where(e1 == e, prefix[e:e + 1, :], r1)
    base_ref[...] = base_ref[...] + jnp.sum(sel_m, axis=1, keepdims=True)
    cnt_ref[...] = base_ref[...]
    zrow = jnp.zeros((1, tm), jnp.int32)
    route_ref[step] = jnp.concatenate(
        [e0, e1, r0.astype(jnp.int32), r1.astype(jnp.int32), zrow, zrow, zrow, zrow], axis=0)

    @pl.when(step == pl.num_programs(0) - 1)
    def _():
        counts = base_ref[:, 0:1]
        padded = jnp.ceil(counts * (1.0 / SLOT_TILE)) * SLOT_TILE
        routes = route_ref[...]
        ea, eb = routes[:, 0, :], routes[:, 1, :]
        da, db = routes[:, 2, :], routes[:, 3, :]
        start = jnp.zeros((1, 1), F32)
        for e in range(N_EXPERTS):
            s_e = start.astype(jnp.int32)
            da = da + jnp.where(ea == e, s_e, 0)
            db = db + jnp.where(eb == e, s_e, 0)
            start = start + padded[e:e + 1, :]
        dest_ref[:, 0, :] = jnp.concatenate([da, db], axis=1)

    wcol_ref[...] = jnp.concatenate([w0, w1, jnp.zeros((LANES - 2, tm), F32)], axis=0).T


def _outproj(oas, las, ob, oc, gates, xf, pa, pb, pc, wo, gffn, wr_hi, wr_lo, rbias):
    T = xf.shape[0]
    tm = TOK_TILE
    row = lambda w: pl.BlockSpec((tm, w), lambda i: (i, 0))
    ins, specs = [], []
    for o, l in zip(oas, las):
        ins += [o, l]
        specs += [pl.BlockSpec((2, tm, LANES), lambda i: (0, i, 0))] * 2
    ins += [ob, oc, gates, xf, pa, pb, pc, wo, gffn, wr_hi, wr_lo, rbias]
    specs += [row(B_V), row(C_HEADS * C_V), row(3 * D_MODEL), row(D_MODEL)]
    specs += [_const_spec(a.shape) for a in (pa, pb, pc, wo, gffn, wr_hi, wr_lo, rbias)]
    return pl.pallas_call(
        _outproj_kernel,
        grid=(T // tm,),
        in_specs=specs,
        out_specs=[row(D_MODEL), pl.BlockSpec((tm * ROW_SUB, LANES), lambda i: (i, 0)),
                   pl.BlockSpec((T // tm, 1, 2 * tm), lambda i: (0, 0, 0)),
                   row(LANES), pl.BlockSpec((N_EXPERTS, LANES), lambda i: (0, 0))],
        out_shape=[jax.ShapeDtypeStruct((T, D_MODEL), F32), jax.ShapeDtypeStruct((T * ROW_SUB, LANES), F32),
                   jax.ShapeDtypeStruct((T // tm, 1, 2 * tm), jnp.int32), jax.ShapeDtypeStruct((T, LANES), F32),
                   jax.ShapeDtypeStruct((N_EXPERTS, LANES), F32)],
        scratch_shapes=[pltpu.VMEM((N_EXPERTS, LANES), F32), pltpu.VMEM((T // tm, 8, tm), jnp.int32)],
        compiler_params=_params(("arbitrary",)),
        name="outproj",
    )(*ins)


def _slot_plan(counts, n_tok):
    padded = (counts + SLOT_TILE - 1) // SLOT_TILE * SLOT_TILE
    ends = jnp.cumsum(padded)
    offsets = ends - padded
    max_tiles = (2 * n_tok) // SLOT_TILE + N_EXPERTS
    n_tiles = ends[-1:] // SLOT_TILE
    tile_start = jnp.minimum(jnp.arange(max_tiles, dtype=jnp.int32) * SLOT_TILE, ends[-1] - SLOT_TILE)
    tile_expert = jnp.sum((ends[None, :] <= tile_start[:, None]).astype(jnp.int32), axis=1)
    i32 = lambda a: a.astype(jnp.int32)
    return i32(offsets), i32(padded), i32(tile_expert), i32(n_tiles), max_tiles


def _to_row_tiles(ref, x):
    n = x.shape[0]
    for j in range(ROW_SUB):
        ref[pl.ds(j, n, stride=ROW_SUB), :] = x[:, j * LANES:(j + 1) * LANES]


def _from_row_tiles(ref, n):
    return jnp.concatenate([ref[pl.ds(j, n, stride=ROW_SUB), :] for j in range(ROW_SUB)], axis=1)


def _row_tile(ref, r):
    return ref.at[pl.ds(pl.multiple_of(r * ROW_SUB, ROW_SUB), ROW_SUB)]


def _dispatch_kernel(off_ref, pad_ref, nt_ref, dest_ref, hrow_ref, xs_ref, stage, zero_ref, load_sem, row_sem, zsem):
    tm = TOK_TILE
    i = pl.program_id(0)
    n = pl.num_programs(0)
    tok_rows = tm * ROW_SUB
    tile_rows = SLOT_TILE * ROW_SUB
    max_tiles = xs_ref.shape[0] // tile_rows
    slots = stage.shape[0]

    def zero_tile(t):
        return pltpu.make_async_copy(
            zero_ref, xs_ref.at[pl.ds(pl.multiple_of(t * tile_rows, tile_rows), tile_rows)], zsem)

    def load(t, slot):
        src = hrow_ref.at[pl.ds(pl.multiple_of(t * tok_rows, tok_rows), tok_rows)]
        return pltpu.make_async_copy(src, stage.at[slot], load_sem.at[slot])

    def wait_rows(slot):
        for _ in range(2):
            pltpu.make_async_copy(stage.at[slot], xs_ref.at[pl.ds(0, tok_rows)], row_sem.at[slot]).wait()

    @pl.when(i == 0)
    def _():
        zero_ref[...] = jnp.zeros(zero_ref.shape, F32)

        def fill(t, c):
            zero_tile(t).start()
            return c

        def drain(t, c):
            zero_tile(t).wait()
            return c

        lax.fori_loop(nt_ref[0], max_tiles, fill, 0)
        lax.fori_loop(nt_ref[0], max_tiles, drain, 0)
        for e in range(N_EXPERTS):
            @pl.when(pad_ref[e] > 0)
            def _():
                zero_tile((off_ref[e] + pad_ref[e]) // SLOT_TILE - 1).start()
        for e in range(N_EXPERTS):
            @pl.when(pad_ref[e] > 0)
            def _():
                zero_tile(0).wait()
        load(0, 0).start()

    slot = i % slots
    nxt = (i + 1) % slots

    @pl.when(i >= slots - 1)
    def _():
        wait_rows(nxt)

    @pl.when(i + 1 < n)
    def _():
        load(i + 1, nxt).start()

    load(i, slot).wait()

    def issue(g, c):
        rows = [g * ISSUE_GROUP + u for u in range(ISSUE_GROUP)]
        dsts = [(dest_ref[r], dest_ref[tm + r]) for r in rows]
        for r, (d0, d1) in zip(rows, dsts):
            src = _row_tile(stage.at[slot], r)
            pltpu.make_async_copy(src, _row_tile(xs_ref, d0), row_sem.at[slot]).start(priority=0)
            pltpu.make_async_copy(src, _row_tile(xs_ref, d1), row_sem.at[slot]).start(priority=1)
        return c

    lax.fori_loop(0, tm // ISSUE_GROUP, issue, 0)

    @pl.when(i == n - 1)
    def _():
        for back in range(slots - 2, -1, -1):
            @pl.when(i >= back)
            def _():
                wait_rows((i - back) % slots)


def _dispatch(offsets, padded, n_tiles, dest, hrow, n_slots):
    tm = TOK_TILE
    stage_slots = 3
    gs = pltpu.PrefetchScalarGridSpec(
        num_scalar_prefetch=3,
        grid=(dest.shape[0],),
        in_specs=[pl.BlockSpec((None, None, 2 * tm), lambda i, o, p, n: (i, 0, 0), memory_space=pltpu.SMEM),
                  pl.BlockSpec(memory_space=pl.ANY)],
        out_specs=pl.BlockSpec(memory_space=pl.ANY),
        scratch_shapes=[pltpu.VMEM((stage_slots, tm * ROW_SUB, LANES), F32),
                        pltpu.VMEM((SLOT_TILE * ROW_SUB, LANES), F32),
                        pltpu.SemaphoreType.DMA((stage_slots,)), pltpu.SemaphoreType.DMA((stage_slots,)),
                        pltpu.SemaphoreType.DMA],
    )
    return pl.pallas_call(
        _dispatch_kernel, grid_spec=gs,
        out_shape=jax.ShapeDtypeStruct((n_slots * ROW_SUB, LANES), F32),
        compiler_params=_params(("arbitrary",)),
        name="dispatch",
    )(offsets, padded, n_tiles, dest, hrow)


def _ffn_kernel(te_ref, nt_ref, xs_ref, wg_ref, wu_ref, wd_ref, ys_ref, wg_b, wu_b, wd_b):
    i = pl.program_id(0)
    used = i < nt_ref[0]

    @pl.when(jnp.logical_not(used))
    def _():
        ys_ref[...] = jnp.zeros(ys_ref.shape, F32)

    @pl.when(used & ((i == 0) | (te_ref[i] != te_ref[jnp.maximum(i - 1, 0)])))
    def _():
        wg_b[...] = wg_ref[0].astype(BF16)
        wu_b[...] = wu_ref[0].astype(BF16)
        wd_b[...] = wd_ref[0].astype(BF16)

    @pl.when(used)
    def _():
        x = _from_row_tiles(xs_ref, SLOT_TILE).astype(BF16)
        a = _dot(x, wg_b[...])
        b = _dot(x, wu_b[...])
        act = a * _sigmoid(a) * b
        _to_row_tiles(ys_ref, _dot(act.astype(BF16), wd_b[...]))


def _ffn(tile_expert, n_tiles, xs, wg, wu, wd, layer, max_tiles):
    rows = SLOT_TILE * ROW_SUB
    used = lambda i, te, nt: (jnp.minimum(i, nt[0] - 1), 0)
    wspec = lambda shape: pl.BlockSpec((None, 1) + shape, lambda i, te, nt: (layer, te[i], 0, 0))
    gs = pltpu.PrefetchScalarGridSpec(
        num_scalar_prefetch=2,
        grid=(max_tiles,),
        in_specs=[pl.BlockSpec((rows, LANES), used),
                  wspec((D_MODEL, D_EXPERT)), wspec((D_MODEL, D_EXPERT)), wspec((D_EXPERT, D_MODEL))],
        out_specs=pl.BlockSpec((rows, LANES), lambda i, te, nt: (i, 0)),
        scratch_shapes=[pltpu.VMEM((D_MODEL, D_EXPERT), BF16), pltpu.VMEM((D_MODEL, D_EXPERT), BF16),
                        pltpu.VMEM((D_EXPERT, D_MODEL), BF16)],
    )
    return pl.pallas_call(
        _ffn_kernel, grid_spec=gs,
        out_shape=jax.ShapeDtypeStruct(xs.shape, F32),
        compiler_params=_params(("arbitrary",)),
        name="expert_ffn",
    )(tile_expert, n_tiles, xs, wg, wu, wd)


def _combine_kernel(dest_ref, next_dest_ref, wcol_ref, x_ref, gfin_ref, ys_ref, o_ref, buf, sem, *, final):
    tm = TOK_TILE
    i = pl.program_id(0)
    last = pl.num_programs(0) - 1

    def gather(d_ref, slot):
        def issue(g, c):
            rows = [g * ISSUE_GROUP + u for u in range(ISSUE_GROUP)]
            slots = [(d_ref[r], d_ref[tm + r]) for r in rows]
            for r, (d0, d1) in zip(rows, slots):
                pltpu.make_async_copy(_row_tile(ys_ref, d0), _row_tile(buf.at[slot, 0], r), sem.at[slot]).start(priority=0)
                pltpu.make_async_copy(_row_tile(ys_ref, d1), _row_tile(buf.at[slot, 1], r), sem.at[slot]).start(priority=1)
            return c

        lax.fori_loop(0, tm // ISSUE_GROUP, issue, 0)

    slot = i % 2

    @pl.when(i == 0)
    def _():
        gather(dest_ref, 0)

    @pl.when(i < last)
    def _():
        gather(next_dest_ref, 1 - slot)

    whole = pl.ds(0, tm * ROW_SUB)
    for k in range(2):
        pltpu.make_async_copy(ys_ref.at[whole], buf.at[slot, k], sem.at[slot]).wait()
    w = wcol_ref[...]
    y = (x_ref[...] + w[:, 0:1] * _from_row_tiles(buf.at[slot, 0], tm)
         + w[:, 1:2] * _from_row_tiles(buf.at[slot, 1], tm))
    if final:
        y = _rms(y, gfin_ref[...])
    o_ref[...] = y


def _combine(dest, wcol, xf, gfin, ys, final):
    tm = TOK_TILE
    n = dest.shape[0]
    gs = pl.GridSpec(
        grid=(n,),
        in_specs=[pl.BlockSpec((None, None, 2 * tm), lambda i: (i, 0, 0), memory_space=pltpu.SMEM),
                  pl.BlockSpec((None, None, 2 * tm), lambda i: (jnp.minimum(i + 1, n - 1), 0, 0),
                               memory_space=pltpu.SMEM),
                  pl.BlockSpec((tm, LANES), lambda i: (i, 0)),
                  pl.BlockSpec((tm, D_MODEL), lambda i: (i, 0)),
                  pl.BlockSpec((1, D_MODEL), lambda i: (0, 0)),
                  pl.BlockSpec(memory_space=pl.ANY)],
        out_specs=pl.BlockSpec((tm, D_MODEL), lambda i: (i, 0)),
        scratch_shapes=[pltpu.VMEM((2, 2, tm * ROW_SUB, LANES), F32), pltpu.SemaphoreType.DMA((2,))],
    )
    return pl.pallas_call(
        functools.partial(_combine_kernel, final=final), grid_spec=gs,
        out_shape=jax.ShapeDtypeStruct(xf.shape, F32),
        compiler_params=_params(("arbitrary",)),
        name="combine",
    )(dest, dest, wcol, xf, gfin, ys)


def _alibi_slopes(n):
    return [2.0 ** (-8.0 * h / n) for h in range(1, n + 1)]


def _rope_tables(seq):
    half = C_ROPE // 2
    inv = ROPE_THETA ** (-jnp.arange(half, dtype=F32) * 2.0 / C_ROPE)
    ang = jnp.arange(seq, dtype=F32)[:, None] * inv[None, :]
    cos, sin = jnp.cos(ang), jnp.sin(ang)
    ones = jnp.ones((seq, C_NOPE), F32)
    zeros_n = jnp.zeros((seq, C_NOPE), F32)
    tail = jnp.zeros((seq, C_PAD - C_NOPE - C_ROPE), F32)
    ta = jnp.concatenate([ones, cos, cos, tail], axis=1)
    tb = jnp.concatenate([zeros_n, -sin, sin, tail], axis=1)
    return ta, tb


def _pack_in_weight(w):
    w = w.astype(BF16)
    a_b_lat = w[:, 0:4224]
    kr = w[:, 4224:4256]
    gate = w[:, 4256:]
    half = C_ROPE // 2
    z = lambda n: jnp.zeros((w.shape[0], n), w.dtype)
    kr_main = jnp.concatenate([z(C_NOPE), kr, z(C_PAD - C_NOPE - C_ROPE)], axis=1)
    kr_swap = jnp.concatenate([z(C_NOPE), kr[:, half:], kr[:, :half], z(C_PAD - C_NOPE - C_ROPE)], axis=1)
    return jnp.concatenate([a_b_lat, kr_main, kr_swap, gate], axis=1)


def _pack_latent_weights(w_uq, w_ukv):
    dq = C_NOPE + C_ROPE
    half = C_ROPE // 2
    rows_q = w_uq.shape[0]
    zq = lambda n: jnp.zeros((rows_q, n), w_uq.dtype)
    main, swap = [], []
    for h in range(C_HEADS):
        blk = w_uq[:, h * dq:(h + 1) * dq]
        main += [blk, zq(C_PAD - dq)]
        swap += [zq(C_NOPE), blk[:, C_NOPE + half:], blk[:, C_NOPE:C_NOPE + half], zq(C_PAD - dq)]
    wq2 = jnp.concatenate(main + swap, axis=1).astype(BF16)
    rows_k = w_ukv.shape[0]
    zk = jnp.zeros((rows_k, C_PAD - C_NOPE), w_ukv.dtype)
    dkv = C_NOPE + C_V
    kparts, vparts = [], []
    for h in range(C_HEADS):
        blk = w_ukv[:, h * dkv:(h + 1) * dkv]
        kparts += [blk[:, :C_NOPE], zk]
        vparts.append(blk[:, C_NOPE:])
    wkv2 = jnp.concatenate(kparts + vparts, axis=1).astype(BF16)
    return wq2, wkv2


def kernel(x, w_in, g_mix, p_a, p_b, p_c, w_o, g_q, w_uq, g_kv, w_ukv, lam_q1, lam_k1, lam_q2, lam_k2,
           g_sub, g_ffn, w_router, router_bias, w_gate, w_up, w_down, g_final):
    batch, seq, _ = x.shape
    xf = x.reshape(batch * seq, D_MODEL)
    ta, tb = _rope_tables(seq)
    slopes_a = _alibi_slopes(A_HEADS)
    wr = w_router.T
    wr_hi = wr.astype(BF16)
    wr_lo = (wr - wr_hi.astype(F32)).astype(BF16)
    rbias = router_bias.reshape(N_EXPERTS, 1)
    gfin = g_final.reshape(1, D_MODEL)

    for l in range(DEPTH):
        w_pack = _pack_in_weight(w_in[l])
        wq2, wkv2 = _pack_latent_weights(w_uq[l], w_ukv[l])
        qkva, qkvb, qc, kc, vc, gates = _inproj(
            xf, g_mix[l].reshape(1, -1), w_pack, g_q[l].reshape(1, -1), g_kv[l].reshape(1, -1),
            wq2, wkv2, ta, tb, seq)

        oas, las = [], []
        for g, (_, dilation) in enumerate(DIL_PAIRS):
            o, lse = _dilated(qkva, g, dilation, slopes_a[g * A_SLOTS:(g + 1) * A_SLOTS], batch, seq)
            oas.append(o)
            las.append(lse)

        lam_init = 0.8 - 0.6 * math.exp(-0.3 * l)
        lam_rows = jnp.stack([lam_q1[l], lam_k1[l], lam_q2[l], lam_k2[l]])
        ob = _diff(qkvb, lam_rows, g_sub[l].reshape(1, -1), lam_init, batch, seq)
        oc = _latent(qc, kc, vc, batch, seq)

        xf, hrow, dest, wcol, cnt = _outproj(oas, las, ob, oc, gates, xf,
                                            p_a[l].astype(BF16), p_b[l].astype(BF16), p_c[l].astype(BF16),
                                            w_o[l].astype(BF16), g_ffn[l].reshape(1, -1), wr_hi, wr_lo, rbias)
        offsets, padded, tile_expert, n_tiles, max_tiles = _slot_plan(cnt[:, 0].astype(jnp.int32), batch * seq)
        xs = _dispatch(offsets, padded, n_tiles, dest, hrow, max_tiles * SLOT_TILE)
        ys = _ffn(tile_expert, n_tiles, xs, w_gate, w_up, w_down, l, max_tiles)
        xf = _combine(dest, wcol, xf, gfin, ys, final=(l == DEPTH - 1))
    return xf.reshape(batch, seq, D_MODEL)
```

```python
import functools
import math

import jax
import jax.numpy as jnp
import numpy as np
from jax import lax
from jax.experimental import pallas as pl
from jax.experimental.pallas import tpu as pltpu

F32 = jnp.float32
BF16 = jnp.bfloat16

D_MODEL = 1024
DEPTH = 2
DIL_PAIRS = ((128, 1), (512, 4), (2048, 16))
A_SLOTS = 4
A_HEAD_DIM = 64
A_GROUP = A_SLOTS * A_HEAD_DIM
A_QKV = 3 * A_GROUP
A_HEADS = 12
A_STEPS = 128
B_HEADS = 4
B_HEAD_DIM = 64
B_QK = 512
B_V = 512
C_HEADS = 6
C_NOPE = 64
C_ROPE = 32
C_V = 64
C_Q_LORA = 256
C_KV_LORA = 128
C_PAD = 128
ROPE_THETA = 10000.0
N_EXPERTS = 16
N_EXPERT_GROUPS = 4
EXPERTS_PER_GROUP = 4
D_EXPERT = 512
RMS_EPS = 1e-6
NEG_INF = -1e30

LANES = 128
LOG2E = 1.4426950408889634
VMEM_LIMIT = 56 * 1024 * 1024

COL_A = 0
COL_B = COL_A + 3 * A_QKV
COL_CQ = COL_B + 2 * B_QK + B_V
COL_CKV = COL_CQ + C_Q_LORA
COL_KRM = COL_CKV + C_KV_LORA
COL_KRS = COL_KRM + LANES
COL_GATE = COL_KRS + LANES
COL_END = COL_GATE + 3 * D_MODEL


def _rms(x, g):
    return x * lax.rsqrt(jnp.mean(x * x, axis=-1, keepdims=True) + RMS_EPS) * g


def _sigmoid(z):
    return 1.0 / (1.0 + jnp.exp(-z))


def _dot(a, b):
    return jnp.dot(a, b, preferred_element_type=F32)


def _dot_nt(a, b):
    return lax.dot_general(a, b, (((1,), (1,)), ((), ())), preferred_element_type=F32)


def _params(sem):
    return pltpu.CompilerParams(dimension_semantics=sem, vmem_limit_bytes=VMEM_LIMIT)


def _const_spec(shape):
    nd = len(shape)
    return pl.BlockSpec(shape, lambda *_: (0,) * nd, pipeline_mode=pl.Buffered(1))


def _inproj_kernel(x_ref, gmix_ref, w_ref, wt_ref, gq_ref, gkv_ref, wq2_ref, wkv2_ref, ta_ref, tb_ref,
                   oa_ref, ob_ref, qc_ref, kc_ref, vc_ref, og_ref):
    u = _rms(x_ref[...], gmix_ref[...]).astype(BF16)

    def mm(c0, c1):
        if c1 <= COL_KRM:
            return _dot(u, w_ref[:, c0:c1])
        if c0 >= COL_KRM:
            return _dot(u, wt_ref[:, c0 - COL_KRM:c1 - COL_KRM])
        return jnp.concatenate([_dot(u, w_ref[:, c0:COL_KRM]), _dot(u, wt_ref[:, 0:c1 - COL_KRM])], axis=1)

    for comp in range(3):
        y = mm(COL_A + comp * A_QKV, COL_A + (comp + 1) * A_QKV)
        if comp == 0:
            y = y * A_HEAD_DIM ** -0.5
        for g in range(len(DIL_PAIRS)):
            for hf in range(2):
                c0 = g * A_GROUP + hf * LANES
                oa_ref[(g * 3 + comp) * 2 + hf] = y[:, c0:c0 + LANES]

    b_scale = B_HEAD_DIM ** -0.5 * LOG2E
    ob_ref[:, 0:B_QK] = (mm(COL_B, COL_B + B_QK) * b_scale).astype(BF16)
    ob_ref[:, B_QK:2 * B_QK] = mm(COL_B + B_QK, COL_B + 2 * B_QK).astype(BF16)
    ob_ref[:, 2 * B_QK:2 * B_QK + B_V] = mm(COL_B + 2 * B_QK, COL_CQ).astype(BF16)

    lat = mm(COL_CQ, COL_GATE)
    cqn = _rms(lat[:, 0:C_Q_LORA], gq_ref[...]).astype(BF16)
    ckvn = _rms(lat[:, C_Q_LORA:C_Q_LORA + C_KV_LORA], gkv_ref[...]).astype(BF16)
    krm = lat[:, COL_KRM - COL_CQ:COL_KRS - COL_CQ]
    krs = lat[:, COL_KRS - COL_CQ:COL_GATE - COL_CQ]
    q2 = _dot(cqn, wq2_ref[...])
    kv2 = _dot(ckvn, wkv2_ref[...])
    ta = ta_ref[...]
    tb = tb_ref[...]
    krot = krm * ta + krs * tb
    c_scale = (C_NOPE + C_ROPE) ** -0.5 * LOG2E
    nq = C_HEADS * C_PAD
    for h in range(C_HEADS):
        sl = slice(h * C_PAD, (h + 1) * C_PAD)
        qm = q2[:, h * C_PAD:(h + 1) * C_PAD]
        qs = q2[:, nq + h * C_PAD:nq + (h + 1) * C_PAD]
        qc_ref[:, sl] = ((qm * ta + qs * tb) * c_scale).astype(BF16)
        kc_ref[:, sl] = (kv2[:, h * C_PAD:(h + 1) * C_PAD] + krot).astype(BF16)
    vc_ref[...] = kv2[:, nq:nq + C_HEADS * C_V].astype(BF16)

    gchunk = 768
    for j in range(3 * D_MODEL // gchunk):
        z = mm(COL_GATE + j * gchunk, COL_GATE + (j + 1) * gchunk)
        og_ref[:, j * gchunk:(j + 1) * gchunk] = _sigmoid(z).astype(BF16)


def _inproj(xf, gmix, w_all, layer, w_tail, gq, gkv, wq2, wkv2, ta, tb, seq):
    T = xf.shape[0]
    tm = 512
    n_pos_blocks = seq // tm
    row = lambda w: pl.BlockSpec((tm, w), lambda i: (i, 0))
    tab = pl.BlockSpec((tm, LANES), lambda i: (i % n_pos_blocks, 0))
    widths = (2 * B_QK + B_V, C_HEADS * C_PAD, C_HEADS * C_PAD, C_HEADS * C_V, 3 * D_MODEL)
    a_slabs = 3 * A_QKV // LANES
    return pl.pallas_call(
        _inproj_kernel,
        grid=(T // tm,),
        in_specs=[row(D_MODEL), _const_spec((1, D_MODEL)),
                  pl.BlockSpec((None, D_MODEL, COL_KRM), lambda i: (layer, 0, 0), pipeline_mode=pl.Buffered(1)),
                  _const_spec(w_tail.shape), _const_spec((1, C_Q_LORA)), _const_spec((1, C_KV_LORA)),
                  _const_spec(wq2.shape), _const_spec(wkv2.shape), tab, tab],
        out_specs=[pl.BlockSpec((a_slabs, tm, LANES), lambda i: (0, i, 0))] + [row(w) for w in widths],
        out_shape=[jax.ShapeDtypeStruct((a_slabs, T, LANES), F32)]
        + [jax.ShapeDtypeStruct((T, w), BF16) for w in widths],
        compiler_params=_params(("parallel",)),
        name="inproj",
    )(xf, gmix, w_all, w_tail, gq, gkv, wq2, wkv2, ta, tb)


def _dilated_kernel(a_ref, o_ref, l_ref, *, dilation, sub_len, slopes):
    d = dilation
    nb = sub_len // A_STEPS
    win = min(2 * A_STEPS, sub_len)
    stack = A_SLOTS * A_STEPS
    head_lane = lax.broadcasted_iota(jnp.int32, (1, A_GROUP), 1) // A_HEAD_DIM
    row = lax.broadcasted_iota(jnp.int32, (stack, 1), 0)
    slope_col = jnp.zeros((stack, 1), F32)
    for h in range(A_SLOTS):
        slope_col = jnp.where(row // A_STEPS == h, slopes[h] * d, slope_col)
    rel = row % A_STEPS - lax.broadcasted_iota(jnp.int32, (1, win), 1)

    def bias_mask(offset):
        dist = rel + offset
        return jnp.where((dist >= 0) & (dist <= A_STEPS), -slope_col * dist.astype(F32), NEG_INF)

    bm_first = bias_mask(0)
    bm_rest = bias_mask(A_STEPS) if nb > 1 else None

    def rows(first, count):
        return pl.ds(first, count) if d == 1 else pl.ds(first, count, stride=d)

    def load(pair, rws):
        return jnp.concatenate([a_ref[2 * pair, rws, :], a_ref[2 * pair + 1, rws, :]], axis=1).astype(BF16)

    for c in range(d):
        for n in range(nb):
            start = min(max((n - 1) * A_STEPS, 0), sub_len - win)
            bm = bm_first if n * A_STEPS == start else bm_rest
            q_rows = rows(c + n * A_STEPS * d, A_STEPS)
            k_rows = rows(c + start * d, win)
            q = load(0, q_rows)
            k = load(1, k_rows)
            v = load(2, k_rows)
            zero = jnp.zeros_like(q)
            q4 = jnp.concatenate([jnp.where(head_lane == h, q, zero) for h in range(A_SLOTS)], axis=0)
            s = _dot_nt(q4, k) + bm
            m = jnp.max(s, axis=-1, keepdims=True)
            p = jnp.exp(s - m)
            l = jnp.sum(p, axis=-1, keepdims=True)
            o_st = _dot(p.astype(BF16), v) / l
            lse_st = m + jnp.log(l)
            out = jnp.zeros((A_STEPS, A_GROUP), F32)
            lse = jnp.zeros((A_STEPS, A_GROUP), F32)
            for h in range(A_SLOTS):
                own = head_lane == h
                out = jnp.where(own, o_st[h * A_STEPS:(h + 1) * A_STEPS], out)
                lse = jnp.where(own, lse_st[h * A_STEPS:(h + 1) * A_STEPS], lse)
            for hf in range(2):
                o_ref[hf, q_rows, :] = out[:, hf * LANES:(hf + 1) * LANES]
                l_ref[hf, q_rows, :] = lse[:, hf * LANES:(hf + 1) * LANES]


def _dilated(a_slabs, group, dilation, slopes, batch, seq):
    kern = functools.partial(_dilated_kernel, dilation=dilation, sub_len=seq // dilation, slopes=slopes)
    o_spec = pl.BlockSpec((2, seq, LANES), lambda b: (0, b, 0))
    shape = jax.ShapeDtypeStruct((2, batch * seq, LANES), F32)
    return pl.pallas_call(
        kern,
        grid=(batch,),
        in_specs=[pl.BlockSpec((6, seq, LANES), lambda b: (group, b, 0))],
        out_specs=[o_spec, o_spec],
        out_shape=[shape, shape],
        compiler_params=_params(("parallel",)),
        name=f"dilated{group}",
    )(a_slabs)


ATT_TQ = 512
ATT_TK = 512
ATT_SUB = 256


def _tree(op, xs):
    while len(xs) > 1:
        xs = [op(xs[i], xs[i + 1]) if i + 1 < len(xs) else xs[i] for i in range(0, len(xs), 2)]
    return xs[0]


def _attn_kernel(*refs, diff, seq, lam_init):
    if diff:
        lam_ref, gsub_ref = refs[:2]
        refs = refs[2:]
    q_ref, k_ref, v_ref, o_ref, s_a, s_b = refs
    tq, tk, sub = ATT_TQ, ATT_TK, ATT_SUB
    n_sub = tq // sub
    lane = lax.broadcasted_iota(jnp.int32, (1, LANES), 1)
    tri = lax.broadcasted_iota(jnp.int32, (sub, sub), 0) >= lax.broadcasted_iota(jnp.int32, (sub, sub), 1)
    if diff:
        h = pl.program_id(1)
        slope = jnp.exp2(jnp.full((1, 1), -2.0, F32) * (h + 1).astype(F32)) * LOG2E
        lam_rows = lam_ref[...]
        lam = (jnp.exp(jnp.sum(lam_rows[0:1] * lam_rows[1:2], axis=-1, keepdims=True))
               - jnp.exp(jnp.sum(lam_rows[2:3] * lam_rows[3:4], axis=-1, keepdims=True)) + lam_init)
    s_refs = (s_a, s_b)

    for qi in range(seq // tq):
        q0 = qi * tq
        q = q_ref[0, q0:q0 + tq, :]
        if diff:
            zero = jnp.zeros_like(q)
            qs = (jnp.where(lane < B_HEAD_DIM, q, zero), jnp.where(lane >= B_HEAD_DIM, q, zero))
        else:
            qs = (q[:, :C_PAD], q[:, C_PAD:])
        work = [(0, tq, c * tk, tk, False) for c in range(q0 // tk)]
        for i in range(n_sub):
            for j in range(i + 1):
                work.append((i * sub, sub, q0 + j * sub, sub, i == j))

        m = [[jnp.full((sub, LANES), NEG_INF, F32)] * n_sub for _ in range(2)]
        for r0, nr, k0, w, masked in work:
            k = k_ref[0, k0:k0 + w, :]
            for idx in range(2):
                kk = k if diff else k[:, idx * C_PAD:(idx + 1) * C_PAD]
                s = _dot_nt(qs[idx][r0:r0 + nr], kk)
                if diff:
                    kpos = k0 - q0 + lax.broadcasted_iota(jnp.int32, (1, w), 1)
                    s = s + slope * kpos.astype(F32)
                if masked:
                    s = jnp.where(tri, s, NEG_INF)
                s_refs[idx][r0:r0 + nr, k0:k0 + w] = s
                for part in range(nr // sub):
                    blk = r0 // sub + part
                    cols = [s[part * sub:(part + 1) * sub, g * LANES:(g + 1) * LANES] for g in range(w // LANES)]
                    m[idx][blk] = jnp.maximum(m[idx][blk], _tree(jnp.maximum, cols))
        row_max = [[jnp.broadcast_to(jnp.max(mm, axis=-1, keepdims=True), (sub, LANES)) for mm in m[idx]]
                   for idx in range(2)]

        dv = v_ref.shape[-1]
        lsum = [[jnp.zeros((sub, LANES), F32)] * n_sub for _ in range(2)]
        acc_tile = [jnp.zeros((tq, dv), F32)] * 2
        acc_blk = [[jnp.zeros((sub, dv), F32)] * n_sub for _ in range(2)]
        for r0, nr, k0, w, _ in work:
            v = v_ref[0, k0:k0 + w, :]
            for idx in range(2):
                rows = []
                for part in range(nr // sub):
                    blk = r0 // sub + part
                    lo = r0 + part * sub
                    ps = [jnp.exp2(s_refs[idx][lo:lo + sub, k0 + g * LANES:k0 + (g + 1) * LANES] - row_max[idx][blk])
                          for g in range(w // LANES)]
                    lsum[idx][blk] = lsum[idx][blk] + _tree(jnp.add, ps)
                    rows.append(jnp.concatenate([p.astype(BF16) for p in ps], axis=1))
                if nr == tq:
                    acc_tile[idx] = acc_tile[idx] + _dot(jnp.concatenate(rows, axis=0), v)
                else:
                    acc_blk[idx][r0 // sub] = acc_blk[idx][r0 // sub] + _dot(rows[0], v)
        for blk in range(n_sub):
            rows_blk = slice(blk * sub, (blk + 1) * sub)
            oa = (acc_tile[0][rows_blk] + acc_blk[0][blk]) / jnp.sum(lsum[0][blk], axis=-1, keepdims=True)
            ob = (acc_tile[1][rows_blk] + acc_blk[1][blk]) / jnp.sum(lsum[1][blk], axis=-1, keepdims=True)
            if diff:
                out = _rms(oa - lam * ob, gsub_ref[...]) * (1.0 - lam_init)
            else:
                out = jnp.where(lane < C_V, oa, ob)
            o_ref[0, q0 + blk * sub:q0 + (blk + 1) * sub, :] = out.astype(BF16)


def _attn_scratch(seq):
    return [pltpu.VMEM((ATT_TQ, seq), F32)] * 2


def _diff(qkvb, lam_rows, gsub, lam_init, batch, seq):
    x = qkvb.reshape(batch, seq, 2 * B_QK + B_V)
    kern = functools.partial(_attn_kernel, diff=True, seq=seq, lam_init=lam_init)
    out = pl.pallas_call(
        kern,
        grid=(batch, B_HEADS),
        in_specs=[_const_spec(lam_rows.shape), _const_spec(gsub.shape),
                  pl.BlockSpec((1, seq, LANES), lambda b, h: (b, 0, h)),
                  pl.BlockSpec((1, seq, LANES), lambda b, h: (b, 0, B_HEADS + h)),
                  pl.BlockSpec((1, seq, LANES), lambda b, h: (b, 0, 2 * B_HEADS + h))],
        out_specs=pl.BlockSpec((1, seq, LANES), lambda b, h: (b, 0, h)),
        out_shape=jax.ShapeDtypeStruct((batch, seq, B_V), BF16),
        scratch_shapes=_attn_scratch(seq),
        compiler_params=_params(("parallel", "parallel")),
        name="diff_attn",
    )(lam_rows, gsub, x, x, x)
    return out.reshape(batch * seq, B_V)


def _latent(qc, kc, vc, batch, seq):
    kern = functools.partial(_attn_kernel, diff=False, seq=seq, lam_init=0.0)
    out = pl.pallas_call(
        kern,
        grid=(batch, C_HEADS // 2),
        in_specs=[pl.BlockSpec((1, seq, 2 * C_PAD), lambda b, h: (b, 0, h)),
                  pl.BlockSpec((1, seq, 2 * C_PAD), lambda b, h: (b, 0, h)),
                  pl.BlockSpec((1, seq, 2 * C_V), lambda b, h: (b, 0, h))],
        out_specs=pl.BlockSpec((1, seq, 2 * C_V), lambda b, h: (b, 0, h)),
        out_shape=jax.ShapeDtypeStruct((batch, seq, C_HEADS * C_V), BF16),
        scratch_shapes=_attn_scratch(seq),
        compiler_params=_params(("parallel", "parallel")),
        name="latent_attn",
    )(qc.reshape(batch, seq, -1), kc.reshape(batch, seq, -1), vc.reshape(batch, seq, -1))
    return out.reshape(batch * seq, C_HEADS * C_V)


TOK_TILE = 512
SLOT_TILE = 512
ROW_SUB = 8
ISSUE_GROUP = 8
FFN_SUB = 256


def _route(sc, bs):
    gscore = []
    for g in range(N_EXPERT_GROUPS):
        a, b, c, d = bs[4 * g:4 * g + 4]
        hi1, lo1 = jnp.maximum(a, b), jnp.minimum(a, b)
        hi2, lo2 = jnp.maximum(c, d), jnp.minimum(c, d)
        gscore.append(jnp.maximum(hi1, hi2) + jnp.maximum(jnp.minimum(hi1, hi2), jnp.maximum(lo1, lo2)))
    picked = []
    for g in range(N_EXPERT_GROUPS):
        ok = None
        for i in range(N_EXPERT_GROUPS):
            if i == g:
                continue
            c = gscore[g] > gscore[i] if i < g else gscore[g] >= gscore[i]
            ok = c if ok is None else ok & c
        picked.append(ok)
    sel = []
    for g in range(N_EXPERT_GROUPS):
        for j in range(EXPERTS_PER_GROUP):
            vj = bs[4 * g + j]
            rank = jnp.zeros(vj.shape, jnp.int32)
            for i in range(EXPERTS_PER_GROUP):
                if i == j:
                    continue
                vi = bs[4 * g + i]
                ahead = vi >= vj if i < j else vi > vj
                rank = rank + ahead.astype(jnp.int32)
            sel.append(picked[g] & (rank < 2))
    picked_w = [jnp.where(sel[e], sc[e], 0.0) for e in range(N_EXPERTS)]
    total = picked_w[0]
    for e in range(1, N_EXPERTS):
        total = total + picked_w[e]
    return sel, [w / total for w in picked_w]


def _outproj_kernel(oa0, la0, oa1, la1, oa2, la2, ob_ref, oc_ref, gt_ref, x_ref,
                    pa_ref, pb_ref, pc_ref, wo_ref, gffn_ref, wrh_ref, wrl_ref, rb_ref,
                    xn_ref, hrow_ref, dest_ref, wcol_ref, cnt_ref, base_ref, route_ref):
    tm = TOK_TILE
    step = pl.program_id(0)

    @pl.when(step == 0)
    def _():
        base_ref[...] = jnp.zeros(base_ref.shape, F32)

    wide = lambda ref: jnp.concatenate([ref[0], ref[1]], axis=1)
    l0, l1, l2 = wide(la0), wide(la1), wide(la2)
    mx = jnp.maximum(jnp.maximum(l0, l1), l2)
    g0, g1, g2 = jnp.exp(l0 - mx), jnp.exp(l1 - mx), jnp.exp(l2 - mx)
    o_a = (g0 * wide(oa0) + g1 * wide(oa1) + g2 * wide(oa2)) / (g0 + g1 + g2)
    ya = _dot(o_a.astype(BF16), pa_ref[...])
    yb = _dot(ob_ref[...], pb_ref[...])
    yc = _dot(oc_ref[...], pc_ref[...])
    merged = (gt_ref[:, 0:D_MODEL].astype(F32) * ya
              + gt_ref[:, D_MODEL:2 * D_MODEL].astype(F32) * yb
              + gt_ref[:, 2 * D_MODEL:3 * D_MODEL].astype(F32) * yc)
    xn = x_ref[...] + _dot(merged.astype(BF16), wo_ref[...])
    xn_ref[...] = xn
    hn = _rms(xn, gffn_ref[...])
    h_hi = hn.astype(BF16)
    h_lo = (hn - h_hi.astype(F32)).astype(BF16)
    _to_row_tiles(hrow_ref, hn)

    logits = _dot_nt(wrh_ref[...], h_hi) + _dot_nt(wrh_ref[...], h_lo) + _dot_nt(wrl_ref[...], h_hi)
    scores = _sigmoid(logits)
    biased = scores + rb_ref[...]
    sc = [scores[e:e + 1, :] for e in range(N_EXPERTS)]
    bs = [biased[e:e + 1, :] for e in range(N_EXPERTS)]
    sel, w = _route(sc, bs)
    e0 = jnp.full((1, tm), N_EXPERTS, jnp.int32)
    e1 = jnp.full((1, tm), -1, jnp.int32)
    for e in range(N_EXPERTS):
        e0 = jnp.minimum(e0, jnp.where(sel[e], e, N_EXPERTS))
        e1 = jnp.maximum(e1, jnp.where(sel[e], e, -1))
    w0 = jnp.zeros((1, tm), F32)
    w1 = jnp.zeros((1, tm), F32)
    sub = lax.broadcasted_iota(jnp.int32, (N_EXPERTS, 1), 0)
    sel_m = jnp.zeros((N_EXPERTS, tm), F32)
    for e in range(N_EXPERTS):
        w0 = jnp.where(e0 == e, w[e], w0)
        w1 = jnp.where(e1 == e, w[e], w1)
        sel_m = jnp.where((sub == e) & sel[e], 1.0, sel_m)
    upper = lax.broadcasted_iota(jnp.int32, (tm, tm), 0) < lax.broadcasted_iota(jnp.int32, (tm, tm), 1)
    prefix = _dot(sel_m.astype(BF16), upper.astype(BF16)) + base_ref[:, 0:1]
    r0 = jnp.zeros((1, tm), F32)
    r1 = jnp.zeros((1, tm), F32)
    for e in range(N_EXPERTS):
        r0 = jnp.where(e0 == e, prefix[e:e + 1, :], r0)
        r1 = jnp.where(e1 == e, prefix[e:e + 1, :], r1)
    base_ref[...] = base_ref[...] + jnp.sum(sel_m, axis=1, keepdims=True)
    cnt_ref[...] = base_ref[...]
    zrow = jnp.zeros((1, tm), jnp.int32)
    route_ref[step] = jnp.concatenate(
        [e0, e1, r0.astype(jnp.int32), r1.astype(jnp.int32), zrow, zrow, zrow, zrow], axis=0)

    @pl.when(step == pl.num_programs(0) - 1)
    def _():
        counts = base_ref[:, 0:1]
        padded = jnp.ceil(counts * (1.0 / SLOT_TILE)) * SLOT_TILE
        routes = route_ref[...]
        ea, eb = routes[:, 0, :], routes[:, 1, :]
        da, db = routes[:, 2, :], routes[:, 3, :]
        start = jnp.zeros((1, 1), F32)
        for e in range(N_EXPERTS):
            s_e = start.astype(jnp.int32)
            da = da + jnp.where(ea == e, s_e, 0)
            db = db + jnp.where(eb == e, s_e, 0)
            start = start + padded[e:e + 1, :]
        dest_ref[:, 0, :] = jnp.concatenate([da, db], axis=1)

    wcol_ref[...] = jnp.concatenate([w0, w1, jnp.zeros((LANES - 2, tm), F32)], axis=0).T


def _outproj(oas, las, ob, oc, gates, xf, pa, pb, pc, wo, gffn, wr_hi, wr_lo, rbias):
    T = xf.shape[0]
    tm = TOK_TILE
    row = lambda w: pl.BlockSpec((tm, w), lambda i: (i, 0))
    ins, specs = [], []
    for o, l in zip(oas, las):
        ins += [o, l]
        specs += [pl.BlockSpec((2, tm, LANES), lambda i: (0, i, 0))] * 2
    ins += [ob, oc, gates, xf, pa, pb, pc, wo, gffn, wr_hi, wr_lo, rbias]
    specs += [row(B_V), row(C_HEADS * C_V), row(3 * D_MODEL), row(D_MODEL)]
    specs += [_const_spec(a.shape) for a in (pa, pb, pc, wo, gffn, wr_hi, wr_lo, rbias)]
    return pl.pallas_call(
        _outproj_kernel,
        grid=(T // tm,),
        in_specs=specs,
        out_specs=[row(D_MODEL), pl.BlockSpec((tm * ROW_SUB, LANES), lambda i: (i, 0)),
                   pl.BlockSpec((T // tm, 1, 2 * tm), lambda i: (0, 0, 0)),
                   row(LANES), pl.BlockSpec((N_EXPERTS, LANES), lambda i: (0, 0))],
        out_shape=[jax.ShapeDtypeStruct((T, D_MODEL), F32), jax.ShapeDtypeStruct((T * ROW_SUB, LANES), F32),
                   jax.ShapeDtypeStruct((T // tm, 1, 2 * tm), jnp.int32), jax.ShapeDtypeStruct((T, LANES), F32),
                   jax.ShapeDtypeStruct((N_EXPERTS, LANES), F32)],
        scratch_shapes=[pltpu.VMEM((N_EXPERTS, LANES), F32), pltpu.VMEM((T // tm, 8, tm), jnp.int32)],
        compiler_params=_params(("arbitrary",)),
        name="outproj",
    )(*ins)


def _slot_plan(counts, n_tok):
    padded = (counts + SLOT_TILE - 1) // SLOT_TILE * SLOT_TILE
    ends = jnp.cumsum(padded)
    offsets = ends - padded
    max_tiles = (2 * n_tok) // SLOT_TILE + N_EXPERTS
    n_tiles = ends[-1:] // SLOT_TILE
    tile_start = jnp.minimum(jnp.arange(max_tiles, dtype=jnp.int32) * SLOT_TILE, ends[-1] - SLOT_TILE)
    tile_expert = jnp.sum((ends[None, :] <= tile_start[:, None]).astype(jnp.int32), axis=1)
    i32 = lambda a: a.astype(jnp.int32)
    return i32(offsets), i32(padded), i32(tile_expert), i32(n_tiles), max_tiles


def _to_row_tiles(ref, x):
    n = x.shape[0]
    for j in range(ROW_SUB):
        ref[pl.ds(j, n, stride=ROW_SUB), :] = x[:, j * LANES:(j + 1) * LANES]


def _from_row_tiles(ref, n):
    return jnp.concatenate([ref[pl.ds(j, n, stride=ROW_SUB), :] for j in range(ROW_SUB)], axis=1)


def _row_tile(ref, r):
    return ref.at[pl.ds(pl.multiple_of(r * ROW_SUB, ROW_SUB), ROW_SUB)]


def _dispatch_kernel(off_ref, pad_ref, nt_ref, dest_ref, hrow_ref, xs_ref, stage, zero_ref, load_sem, row_sem, zsem):
    tm = TOK_TILE
    i = pl.program_id(0)
    n = pl.num_programs(0)
    tok_rows = tm * ROW_SUB
    tile_rows = SLOT_TILE * ROW_SUB
    max_tiles = xs_ref.shape[0] // tile_rows
    slots = stage.shape[0]

    def zero_tile(t):
        return pltpu.make_async_copy(
            zero_ref, xs_ref.at[pl.ds(pl.multiple_of(t * tile_rows, tile_rows), tile_rows)], zsem)

    def load(t, slot):
        src = hrow_ref.at[pl.ds(pl.multiple_of(t * tok_rows, tok_rows), tok_rows)]
        return pltpu.make_async_copy(src, stage.at[slot], load_sem.at[slot])

    def wait_rows(slot):
        for _ in range(2):
            pltpu.make_async_copy(stage.at[slot], xs_ref.at[pl.ds(0, tok_rows)], row_sem.at[slot]).wait()

    @pl.when(i == 0)
    def _():
        zero_ref[...] = jnp.zeros(zero_ref.shape, F32)

        def fill(t, c):
            zero_tile(t).start()
            return c

        def drain(t, c):
            zero_tile(t).wait()
            return c

        lax.fori_loop(nt_ref[0], max_tiles, fill, 0)
        lax.fori_loop(nt_ref[0], max_tiles, drain, 0)
        for e in range(N_EXPERTS):
            @pl.when(pad_ref[e] > 0)
            def _():
                zero_tile((off_ref[e] + pad_ref[e]) // SLOT_TILE - 1).start()
        for e in range(N_EXPERTS):
            @pl.when(pad_ref[e] > 0)
            def _():
                zero_tile(0).wait()
        load(0, 0).start()

    slot = i % slots
    nxt = (i + 1) % slots

    @pl.when(i >= slots - 1)
    def _():
        wait_rows(nxt)

    @pl.when(i + 1 < n)
    def _():
        load(i + 1, nxt).start()

    load(i, slot).wait()

    def issue(g, c):
        rows = [g * ISSUE_GROUP + u for u in range(ISSUE_GROUP)]
        dsts = [(dest_ref[r], dest_ref[tm + r]) for r in rows]
        for r, (d0, d1) in zip(rows, dsts):
            src = _row_tile(stage.at[slot], r)
            pltpu.make_async_copy(src, _row_tile(xs_ref, d0), row_sem.at[slot]).start(priority=0)
            pltpu.make_async_copy(src, _row_tile(xs_ref, d1), row_sem.at[slot]).start(priority=1)
        return c

    lax.fori_loop(0, tm // ISSUE_GROUP, issue, 0)

    @pl.when(i == n - 1)
    def _():
        for back in range(slots - 2, -1, -1):
            @pl.when(i >= back)
            def _():
                wait_rows((i - back) % slots)


def _dispatch(offsets, padded, n_tiles, dest, hrow, n_slots):
    tm = TOK_TILE
    stage_slots = 3
    gs = pltpu.PrefetchScalarGridSpec(
        num_scalar_prefetch=3,
        grid=(dest.shape[0],),
        in_specs=[pl.BlockSpec((None, None, 2 * tm), lambda i, o, p, n: (i, 0, 0), memory_space=pltpu.SMEM),
                  pl.BlockSpec(memory_space=pl.ANY)],
        out_specs=pl.BlockSpec(memory_space=pl.ANY),
        scratch_shapes=[pltpu.VMEM((stage_slots, tm * ROW_SUB, LANES), F32),
                        pltpu.VMEM((SLOT_TILE * ROW_SUB, LANES), F32),
                        pltpu.SemaphoreType.DMA((stage_slots,)), pltpu.SemaphoreType.DMA((stage_slots,)),
                        pltpu.SemaphoreType.DMA],
    )
    return pl.pallas_call(
        _dispatch_kernel, grid_spec=gs,
        out_shape=jax.ShapeDtypeStruct((n_slots * ROW_SUB, LANES), F32),
        compiler_params=_params(("arbitrary",)),
        name="dispatch",
    )(offsets, padded, n_tiles, dest, hrow)


def _ffn_kernel(te_ref, nt_ref, xs_ref, wg_ref, wu_ref, wd_ref, ys_ref, wg_b, wu_b, wd_b):
    i = pl.program_id(0)
    used = i < nt_ref[0]

    @pl.when(jnp.logical_not(used))
    def _():
        ys_ref[...] = jnp.zeros(ys_ref.shape, F32)

    @pl.when(used & ((i == 0) | (te_ref[i] != te_ref[jnp.maximum(i - 1, 0)])))
    def _():
        wg_b[...] = wg_ref[0].astype(BF16)
        wu_b[...] = wu_ref[0].astype(BF16)
        wd_b[...] = wd_ref[0].astype(BF16)

    @pl.when(used)
    def _():
        for part in range(SLOT_TILE // FFN_SUB):
            rows = pl.ds(part * FFN_SUB * ROW_SUB, FFN_SUB * ROW_SUB)
            x = _from_row_tiles(xs_ref.at[rows], FFN_SUB).astype(BF16)
            a = _dot(x, wg_b[...])
            b = _dot(x, wu_b[...])
            act = a * _sigmoid(a) * b
            _to_row_tiles(ys_ref.at[rows], _dot(act.astype(BF16), wd_b[...]))


def _ffn(tile_expert, n_tiles, xs, wg, wu, wd, layer, max_tiles):
    rows = SLOT_TILE * ROW_SUB
    used = lambda i, te, nt: (jnp.minimum(i, nt[0] - 1), 0)
    wspec = lambda shape: pl.BlockSpec((None, 1) + shape, lambda i, te, nt: (layer, te[i], 0, 0))
    gs = pltpu.PrefetchScalarGridSpec(
        num_scalar_prefetch=2,
        grid=(max_tiles,),
        in_specs=[pl.BlockSpec((rows, LANES), used),
                  wspec((D_MODEL, D_EXPERT)), wspec((D_MODEL, D_EXPERT)), wspec((D_EXPERT, D_MODEL))],
        out_specs=pl.BlockSpec((rows, LANES), lambda i, te, nt: (i, 0)),
        scratch_shapes=[pltpu.VMEM((D_MODEL, D_EXPERT), BF16), pltpu.VMEM((D_MODEL, D_EXPERT), BF16),
                        pltpu.VMEM((D_EXPERT, D_MODEL), BF16)],
    )
    return pl.pallas_call(
        _ffn_kernel, grid_spec=gs,
        out_shape=jax.ShapeDtypeStruct(xs.shape, F32),
        compiler_params=_params(("arbitrary",)),
        name="expert_ffn",
    )(tile_expert, n_tiles, xs, wg, wu, wd)


def _combine_kernel(dest_ref, next_dest_ref, wcol_ref, x_ref, gfin_ref, ys_ref, o_ref, buf, sem, *, final):
    tm = TOK_TILE
    i = pl.program_id(0)
    last = pl.num_programs(0) - 1

    def gather(d_ref, slot):
        def issue(g, c):
            rows = [g * ISSUE_GROUP + u for u in range(ISSUE_GROUP)]
            slots = [(d_ref[r], d_ref[tm + r]) for r in rows]
            for r, (d0, d1) in zip(rows, slots):
                pltpu.make_async_copy(_row_tile(ys_ref, d0), _row_tile(buf.at[slot, 0], r), sem.at[slot]).start(priority=0)
                pltpu.make_async_copy(_row_tile(ys_ref, d1), _row_tile(buf.at[slot, 1], r), sem.at[slot]).start(priority=1)
            return c

        lax.fori_loop(0, tm // ISSUE_GROUP, issue, 0)

    slot = i % 2

    @pl.when(i == 0)
    def _():
        gather(dest_ref, 0)

    @pl.when(i < last)
    def _():
        gather(next_dest_ref, 1 - slot)

    whole = pl.ds(0, tm * ROW_SUB)
    for k in range(2):
        pltpu.make_async_copy(ys_ref.at[whole], buf.at[slot, k], sem.at[slot]).wait()
    w = wcol_ref[...]
    y = (x_ref[...] + w[:, 0:1] * _from_row_tiles(buf.at[slot, 0], tm)
         + w[:, 1:2] * _from_row_tiles(buf.at[slot, 1], tm))
    if final:
        y = _rms(y, gfin_ref[...])
    o_ref[...] = y


def _combine(dest, wcol, xf, gfin, ys, final):
    tm = TOK_TILE
    n = dest.shape[0]
    gs = pl.GridSpec(
        grid=(n,),
        in_specs=[pl.BlockSpec((None, None, 2 * tm), lambda i: (i, 0, 0), memory_space=pltpu.SMEM),
                  pl.BlockSpec((None, None, 2 * tm), lambda i: (jnp.minimum(i + 1, n - 1), 0, 0),
                               memory_space=pltpu.SMEM),
                  pl.BlockSpec((tm, LANES), lambda i: (i, 0)),
                  pl.BlockSpec((tm, D_MODEL), lambda i: (i, 0)),
                  pl.BlockSpec((1, D_MODEL), lambda i: (0, 0)),
                  pl.BlockSpec(memory_space=pl.ANY)],
        out_specs=pl.BlockSpec((tm, D_MODEL), lambda i: (i, 0)),
        scratch_shapes=[pltpu.VMEM((2, 2, tm * ROW_SUB, LANES), F32), pltpu.SemaphoreType.DMA((2,))],
    )
    return pl.pallas_call(
        functools.partial(_combine_kernel, final=final), grid_spec=gs,
        out_shape=jax.ShapeDtypeStruct(xf.shape, F32),
        compiler_params=_params(("arbitrary",)),
        name="combine",
    )(dest, dest, wcol, xf, gfin, ys)


def _alibi_slopes(n):
    return [2.0 ** (-8.0 * h / n) for h in range(1, n + 1)]


def _rope_tables(seq):
    half = C_ROPE // 2
    inv = ROPE_THETA ** (-jnp.arange(half, dtype=F32) * 2.0 / C_ROPE)
    ang = jnp.arange(seq, dtype=F32)[:, None] * inv[None, :]
    cos, sin = jnp.cos(ang), jnp.sin(ang)
    ones = jnp.ones((seq, C_NOPE), F32)
    zeros_n = jnp.zeros((seq, C_NOPE), F32)
    tail = jnp.zeros((seq, C_PAD - C_NOPE - C_ROPE), F32)
    ta = jnp.concatenate([ones, cos, cos, tail], axis=1)
    tb = jnp.concatenate([zeros_n, -sin, sin, tail], axis=1)
    return ta, tb


def _pack_in_tail(w):
    kr = w[:, 4224:4256]
    gate = w[:, 4256:]
    half = C_ROPE // 2
    z = lambda n: jnp.zeros((w.shape[0], n), w.dtype)
    kr_main = jnp.concatenate([z(C_NOPE), kr, z(C_PAD - C_NOPE - C_ROPE)], axis=1)
    kr_swap = jnp.concatenate([z(C_NOPE), kr[:, half:], kr[:, :half], z(C_PAD - C_NOPE - C_ROPE)], axis=1)
    return jnp.concatenate([kr_main, kr_swap, gate], axis=1)


def _pack_latent_weights(w_uq, w_ukv):
    dq = C_NOPE + C_ROPE
    half = C_ROPE // 2
    rows_q = w_uq.shape[0]
    zq = lambda n: jnp.zeros((rows_q, n), w_uq.dtype)
    main, swap = [], []
    for h in range(C_HEADS):
        blk = w_uq[:, h * dq:(h + 1) * dq]
        main += [blk, zq(C_PAD - dq)]
        swap += [zq(C_NOPE), blk[:, C_NOPE + half:], blk[:, C_NOPE:C_NOPE + half], zq(C_PAD - dq)]
    wq2 = jnp.concatenate(main + swap, axis=1).astype(BF16)
    rows_k = w_ukv.shape[0]
    zk = jnp.zeros((rows_k, C_PAD - C_NOPE), w_ukv.dtype)
    dkv = C_NOPE + C_V
    kparts, vparts = [], []
    for h in range(C_HEADS):
        blk = w_ukv[:, h * dkv:(h + 1) * dkv]
        kparts += [blk[:, :C_NOPE], zk]
        vparts.append(blk[:, C_NOPE:])
    wkv2 = jnp.concatenate(kparts + vparts, axis=1).astype(BF16)
    return wq2, wkv2


def kernel(x, w_in, g_mix, p_a, p_b, p_c, w_o, g_q, w_uq, g_kv, w_ukv, lam_q1, lam_k1, lam_q2, lam_k2,
           g_sub, g_ffn, w_router, router_bias, w_gate, w_up, w_down, g_final):
    batch, seq, _ = x.shape
    xf = x.reshape(batch * seq, D_MODEL)
    ta, tb = _rope_tables(seq)
    slopes_a = _alibi_slopes(A_HEADS)
    wr = w_router.T
    wr_hi = wr.astype(BF16)
    wr_lo = (wr - wr_hi.astype(F32)).astype(BF16)
    rbias = router_bias.reshape(N_EXPERTS, 1)
    gfin = g_final.reshape(1, D_MODEL)
    w_in_b = w_in.astype(BF16)

    for l in range(DEPTH):
        w_tail = _pack_in_tail(w_in_b[l])
        wq2, wkv2 = _pack_latent_weights(w_uq[l], w_ukv[l])
        qkva, qkvb, qc, kc, vc, gates = _inproj(
            xf, g_mix[l].reshape(1, -1), w_in_b, l, w_tail, g_q[l].reshape(1, -1), g_kv[l].reshape(1, -1),
            wq2, wkv2, ta, tb, seq)

        oas, las = [], []
        for g, (_, dilation) in enumerate(DIL_PAIRS):
            o, lse = _dilated(qkva, g, dilation, slopes_a[g * A_SLOTS:(g + 1) * A_SLOTS], batch, seq)
            oas.append(o)
            las.append(lse)

        lam_init = 0.8 - 0.6 * math.exp(-0.3 * l)
        lam_rows = jnp.stack([lam_q1[l], lam_k1[l], lam_q2[l], lam_k2[l]])
        ob = _diff(qkvb, lam_rows, g_sub[l].reshape(1, -1), lam_init, batch, seq)
        oc = _latent(qc, kc, vc, batch, seq)

        xf, hrow, dest, wcol, cnt = _outproj(oas, las, ob, oc, gates, xf,
                                            p_a[l].astype(BF16), p_b[l].astype(BF16), p_c[l].astype(BF16),
                                            w_o[l].astype(BF16), g_ffn[l].reshape(1, -1), wr_hi, wr_lo, rbias)
        offsets, padded, tile_expert, n_tiles, max_tiles = _slot_plan(cnt[:, 0].astype(jnp.int32), batch * seq)
        xs = _dispatch(offsets, padded, n_tiles, dest, hrow, max_tiles * SLOT_TILE)
        ys = _ffn(tile_expert, n_tiles, xs, w_gate, w_up, w_down, l, max_tiles)
        xf = _combine(dest, wcol, xf, gfin, ys, final=(l == DEPTH - 1))
    return xf.reshape(batch, seq, D_MODEL)
```

```python
import functools
import math

import jax
import jax.numpy as jnp
import numpy as np
from jax import lax
from jax.experimental import pallas as pl
from jax.experimental.pallas import tpu as pltpu

F32 = jnp.float32
BF16 = jnp.bfloat16

D_MODEL = 1024
DEPTH = 2
DIL_PAIRS = ((128, 1), (512, 4), (2048, 16))
A_SLOTS = 4
A_HEAD_DIM = 64
A_GROUP = A_SLOTS * A_HEAD_DIM
A_QKV = 3 * A_GROUP
A_HEADS = 12
A_STEPS = 128
B_HEADS = 4
B_HEAD_DIM = 64
B_QK = 512
B_V = 512
C_HEADS = 6
C_NOPE = 64
C_ROPE = 32
C_V = 64
C_Q_LORA = 256
C_KV_LORA = 128
C_PAD = 128
ROPE_THETA = 10000.0
N_EXPERTS = 16
N_EXPERT_GROUPS = 4
EXPERTS_PER_GROUP = 4
D_EXPERT = 512
RMS_EPS = 1e-6
NEG_INF = -1e30

LANES = 128
LOG2E = 1.4426950408889634
LN2 = 0.6931471805599453
VMEM_LIMIT = 56 * 1024 * 1024

COL_A = 0
COL_B = COL_A + 3 * A_QKV
COL_CQ = COL_B + 2 * B_QK + B_V
COL_CKV = COL_CQ + C_Q_LORA
COL_KRM = COL_CKV + C_KV_LORA
COL_KRS = COL_KRM + LANES
COL_GATE = COL_KRS + LANES
COL_END = COL_GATE + 3 * D_MODEL


def _rms(x, g):
    return x * lax.rsqrt(jnp.mean(x * x, axis=-1, keepdims=True) + RMS_EPS) * g


def _sigmoid(z):
    return 1.0 / (1.0 + jnp.exp(-z))


def _dot(a, b):
    return jnp.dot(a, b, preferred_element_type=F32)


def _dot_nt(a, b):
    return lax.dot_general(a, b, (((1,), (1,)), ((), ())), preferred_element_type=F32)


def _params(sem):
    return pltpu.CompilerParams(dimension_semantics=sem, vmem_limit_bytes=VMEM_LIMIT)


def _const_spec(shape):
    nd = len(shape)
    return pl.BlockSpec(shape, lambda *_: (0,) * nd, pipeline_mode=pl.Buffered(1))


def _inproj_kernel(x_ref, gmix_ref, w_ref, wt_ref, gq_ref, gkv_ref, wq2_ref, wkv2_ref, ta_ref, tb_ref,
                   oa_ref, ob_ref, qc_ref, kc_ref, vc_ref, og_ref):
    u = _rms(x_ref[...], gmix_ref[...]).astype(BF16)

    def mm(c0, c1):
        if c1 <= COL_KRM:
            return _dot(u, w_ref[:, c0:c1])
        if c0 >= COL_KRM:
            return _dot(u, wt_ref[:, c0 - COL_KRM:c1 - COL_KRM])
        return jnp.concatenate([_dot(u, w_ref[:, c0:COL_KRM]), _dot(u, wt_ref[:, 0:c1 - COL_KRM])], axis=1)

    for comp in range(3):
        y = mm(COL_A + comp * A_QKV, COL_A + (comp + 1) * A_QKV)
        if comp == 0:
            y = y * (A_HEAD_DIM ** -0.5 * LOG2E)
        for g in range(len(DIL_PAIRS)):
            for hf in range(2):
                c0 = g * A_GROUP + hf * LANES
                oa_ref[(g * 3 + comp) * 2 + hf] = y[:, c0:c0 + LANES]

    b_scale = B_HEAD_DIM ** -0.5 * LOG2E
    ob_ref[:, 0:B_QK] = (mm(COL_B, COL_B + B_QK) * b_scale).astype(BF16)
    ob_ref[:, B_QK:2 * B_QK] = mm(COL_B + B_QK, COL_B + 2 * B_QK).astype(BF16)
    ob_ref[:, 2 * B_QK:2 * B_QK + B_V] = mm(COL_B + 2 * B_QK, COL_CQ).astype(BF16)

    lat = mm(COL_CQ, COL_GATE)
    cqn = _rms(lat[:, 0:C_Q_LORA], gq_ref[...]).astype(BF16)
    ckvn = _rms(lat[:, C_Q_LORA:C_Q_LORA + C_KV_LORA], gkv_ref[...]).astype(BF16)
    krm = lat[:, COL_KRM - COL_CQ:COL_KRS - COL_CQ]
    krs = lat[:, COL_KRS - COL_CQ:COL_GATE - COL_CQ]
    q2 = _dot(cqn, wq2_ref[...])
    kv2 = _dot(ckvn, wkv2_ref[...])
    ta = ta_ref[...]
    tb = tb_ref[...]
    krot = krm * ta + krs * tb
    c_scale = (C_NOPE + C_ROPE) ** -0.5 * LOG2E
    nq = C_HEADS * C_PAD
    for h in range(C_HEADS):
        sl = slice(h * C_PAD, (h + 1) * C_PAD)
        qm = q2[:, h * C_PAD:(h + 1) * C_PAD]
        qs = q2[:, nq + h * C_PAD:nq + (h + 1) * C_PAD]
        qc_ref[:, sl] = ((qm * ta + qs * tb) * c_scale).astype(BF16)
        kc_ref[:, sl] = (kv2[:, h * C_PAD:(h + 1) * C_PAD] + krot).astype(BF16)
    vc_ref[...] = kv2[:, nq:nq + C_HEADS * C_V].astype(BF16)

    gchunk = 768
    for j in range(3 * D_MODEL // gchunk):
        z = mm(COL_GATE + j * gchunk, COL_GATE + (j + 1) * gchunk)
        og_ref[:, j * gchunk:(j + 1) * gchunk] = _sigmoid(z).astype(BF16)


def _inproj(xf, gmix, w_all, layer, w_tail, gq, gkv, wq2, wkv2, ta, tb, seq):
    T = xf.shape[0]
    tm = 512
    n_pos_blocks = seq // tm
    row = lambda w: pl.BlockSpec((tm, w), lambda i: (i, 0))
    tab = pl.BlockSpec((tm, LANES), lambda i: (i % n_pos_blocks, 0))
    widths = (2 * B_QK + B_V, C_HEADS * C_PAD, C_HEADS * C_PAD, C_HEADS * C_V, 3 * D_MODEL)
    a_slabs = 3 * A_QKV // LANES
    return pl.pallas_call(
        _inproj_kernel,
        grid=(T // tm,),
        in_specs=[row(D_MODEL), _const_spec((1, D_MODEL)),
                  pl.BlockSpec((None, D_MODEL, COL_KRM), lambda i: (layer, 0, 0), pipeline_mode=pl.Buffered(1)),
                  _const_spec(w_tail.shape), _const_spec((1, C_Q_LORA)), _const_spec((1, C_KV_LORA)),
                  _const_spec(wq2.shape), _const_spec(wkv2.shape), tab, tab],
        out_specs=[pl.BlockSpec((a_slabs, tm, LANES), lambda i: (0, i, 0))] + [row(w) for w in widths],
        out_shape=[jax.ShapeDtypeStruct((a_slabs, T, LANES), F32)]
        + [jax.ShapeDtypeStruct((T, w), BF16) for w in widths],
        compiler_params=_params(("parallel",)),
        name="inproj",
    )(xf, gmix, w_all, w_tail, gq, gkv, wq2, wkv2, ta, tb)


def _dilated_kernel(a_ref, o_ref, l_ref, *, dilation, sub_len, slopes):
    d = dilation
    nb = sub_len // A_STEPS
    win = min(2 * A_STEPS, sub_len)
    stack = A_SLOTS * A_STEPS
    head_lane = lax.broadcasted_iota(jnp.int32, (1, A_GROUP), 1) // A_HEAD_DIM
    row = lax.broadcasted_iota(jnp.int32, (stack, 1), 0)
    slope_col = jnp.zeros((stack, 1), F32)
    for h in range(A_SLOTS):
        slope_col = jnp.where(row // A_STEPS == h, slopes[h] * d * LOG2E, slope_col)
    rel = row % A_STEPS - lax.broadcasted_iota(jnp.int32, (1, win), 1)

    def bias_mask(offset):
        dist = rel + offset
        return jnp.where((dist >= 0) & (dist <= A_STEPS), -slope_col * dist.astype(F32), NEG_INF)

    bm_first = bias_mask(0)
    bm_rest = bias_mask(A_STEPS) if nb > 1 else None

    def rows(first, count):
        return pl.ds(first, count) if d == 1 else pl.ds(first, count, stride=d)

    def load(pair, rws):
        return jnp.concatenate([a_ref[2 * pair, rws, :], a_ref[2 * pair + 1, rws, :]], axis=1).astype(BF16)

    for c in range(d):
        for n in range(nb):
            start = min(max((n - 1) * A_STEPS, 0), sub_len - win)
            bm = bm_first if n * A_STEPS == start else bm_rest
            q_rows = rows(c + n * A_STEPS * d, A_STEPS)
            k_rows = rows(c + start * d, win)
            q = load(0, q_rows)
            k = load(1, k_rows)
            v = load(2, k_rows)
            zero = jnp.zeros_like(q)
            q4 = jnp.concatenate([jnp.where(head_lane == h, q, zero) for h in range(A_SLOTS)], axis=0)
            s = _dot_nt(q4, k) + bm
            m = jnp.max(s, axis=-1, keepdims=True)
            p = jnp.exp2(s - m)
            l = jnp.sum(p, axis=-1, keepdims=True)
            pv = _dot(p.astype(BF16), v)
            out = jnp.zeros((A_STEPS, A_GROUP), F32)
            m_sel = jnp.zeros((A_STEPS, A_GROUP), F32)
            l_sel = jnp.ones((A_STEPS, A_GROUP), F32)
            for h in range(A_SLOTS):
                own = head_lane == h
                hrows = slice(h * A_STEPS, (h + 1) * A_STEPS)
                out = jnp.where(own, pv[hrows], out)
                m_sel = jnp.where(own, m[hrows], m_sel)
                l_sel = jnp.where(own, l[hrows], l_sel)
            out = out / l_sel
            lse = (m_sel + jnp.log2(l_sel)) * LN2
            for hf in range(2):
                o_ref[hf, q_rows, :] = out[:, hf * LANES:(hf + 1) * LANES]
                l_ref[hf, q_rows, :] = lse[:, hf * LANES:(hf + 1) * LANES]


def _dilated(a_slabs, group, dilation, slopes, batch, seq):
    kern = functools.partial(_dilated_kernel, dilation=dilation, sub_len=seq // dilation, slopes=slopes)
    o_spec = pl.BlockSpec((2, seq, LANES), lambda b: (0, b, 0))
    shape = jax.ShapeDtypeStruct((2, batch * seq, LANES), F32)
    return pl.pallas_call(
        kern,
        grid=(batch,),
        in_specs=[pl.BlockSpec((6, seq, LANES), lambda b: (group, b, 0))],
        out_specs=[o_spec, o_spec],
        out_shape=[shape, shape],
        compiler_params=_params(("parallel",)),
        name=f"dilated{group}",
    )(a_slabs)


ATT_TQ = 512
ATT_TK = 512
ATT_SUB = 256


def _tree(op, xs):
    while len(xs) > 1:
        xs = [op(xs[i], xs[i + 1]) if i + 1 < len(xs) else xs[i] for i in range(0, len(xs), 2)]
    return xs[0]


def _attn_kernel(*refs, diff, seq, lam_init):
    if diff:
        lam_ref, gsub_ref = refs[:2]
        refs = refs[2:]
    q_ref, k_ref, v_ref, o_ref, s_a, s_b = refs
    tq, tk, sub = ATT_TQ, ATT_TK, ATT_SUB
    n_sub = tq // sub
    lane = lax.broadcasted_iota(jnp.int32, (1, LANES), 1)
    tri = lax.broadcasted_iota(jnp.int32, (sub, sub), 0) >= lax.broadcasted_iota(jnp.int32, (sub, sub), 1)
    if diff:
        h = pl.program_id(1)
        slope = jnp.exp2(jnp.full((1, 1), -2.0, F32) * (h + 1).astype(F32)) * LOG2E
        lam_rows = lam_ref[...]
        lam = (jnp.exp(jnp.sum(lam_rows[0:1] * lam_rows[1:2], axis=-1, keepdims=True))
               - jnp.exp(jnp.sum(lam_rows[2:3] * lam_rows[3:4], axis=-1, keepdims=True)) + lam_init)
    s_refs = (s_a, s_b)

    for qi in range(seq // tq):
        q0 = qi * tq
        q = q_ref[0, q0:q0 + tq, :]
        if diff:
            zero = jnp.zeros_like(q)
            qs = (jnp.where(lane < B_HEAD_DIM, q, zero), jnp.where(lane >= B_HEAD_DIM, q, zero))
        else:
            qs = (q[:, :C_PAD], q[:, C_PAD:])
        work = [(0, tq, c * tk, tk, False) for c in range(q0 // tk)]
        for i in range(n_sub):
            for j in range(i + 1):
                work.append((i * sub, sub, q0 + j * sub, sub, i == j))

        m = [[jnp.full((sub, LANES), NEG_INF, F32)] * n_sub for _ in range(2)]
        for r0, nr, k0, w, masked in work:
            k = k_ref[0, k0:k0 + w, :]
            for idx in range(2):
                kk = k if diff else k[:, idx * C_PAD:(idx + 1) * C_PAD]
                s = _dot_nt(qs[idx][r0:r0 + nr], kk)
                if diff:
                    kpos = k0 - q0 + lax.broadcasted_iota(jnp.int32, (1, w), 1)
                    s = s + slope * kpos.astype(F32)
                if masked:
                    s = jnp.where(tri, s, NEG_INF)
                s_refs[idx][r0:r0 + nr, k0:k0 + w] = s
                for part in range(nr // sub):
                    blk = r0 // sub + part
                    cols = [s[part * sub:(part + 1) * sub, g * LANES:(g + 1) * LANES] for g in range(w // LANES)]
                    m[idx][blk] = jnp.maximum(m[idx][blk], _tree(jnp.maximum, cols))
        row_max = [[jnp.broadcast_to(jnp.max(mm, axis=-1, keepdims=True), (sub, LANES)) for mm in m[idx]]
                   for idx in range(2)]

        dv = v_ref.shape[-1]
        lsum = [[jnp.zeros((sub, LANES), F32)] * n_sub for _ in range(2)]
        acc_tile = [jnp.zeros((tq, dv), F32)] * 2
        acc_blk = [[jnp.zeros((sub, dv), F32)] * n_sub for _ in range(2)]
        for r0, nr, k0, w, _ in work:
            v = v_ref[0, k0:k0 + w, :]
            for idx in range(2):
                rows = []
                for part in range(nr // sub):
                    blk = r0 // sub + part
                    lo = r0 + part * sub
                    ps = [jnp.exp2(s_refs[idx][lo:lo + sub, k0 + g * LANES:k0 + (g + 1) * LANES] - row_max[idx][blk])
                          for g in range(w // LANES)]
                    lsum[idx][blk] = lsum[idx][blk] + _tree(jnp.add, ps)
                    rows.append(jnp.concatenate([p.astype(BF16) for p in ps], axis=1))
                if nr == tq:
                    acc_tile[idx] = acc_tile[idx] + _dot(jnp.concatenate(rows, axis=0), v)
                else:
                    acc_blk[idx][r0 // sub] = acc_blk[idx][r0 // sub] + _dot(rows[0], v)
        for blk in range(n_sub):
            rows_blk = slice(blk * sub, (blk + 1) * sub)
            oa = (acc_tile[0][rows_blk] + acc_blk[0][blk]) / jnp.sum(lsum[0][blk], axis=-1, keepdims=True)
            ob = (acc_tile[1][rows_blk] + acc_blk[1][blk]) / jnp.sum(lsum[1][blk], axis=-1, keepdims=True)
            if diff:
                out = _rms(oa - lam * ob, gsub_ref[...]) * (1.0 - lam_init)
            else:
                out = jnp.where(lane < C_V, oa, ob)
            o_ref[0, q0 + blk * sub:q0 + (blk + 1) * sub, :] = out.astype(BF16)


def _attn_scratch(seq):
    return [pltpu.VMEM((ATT_TQ, seq), F32)] * 2


def _diff(qkvb, lam_rows, gsub, lam_init, batch, seq):
    x = qkvb.reshape(batch, seq, 2 * B_QK + B_V)
    kern = functools.partial(_attn_kernel, diff=True, seq=seq, lam_init=lam_init)
    out = pl.pallas_call(
        kern,
        grid=(batch, B_HEADS),
        in_specs=[_const_spec(lam_rows.shape), _const_spec(gsub.shape),
                  pl.BlockSpec((1, seq, LANES), lambda b, h: (b, 0, h)),
                  pl.BlockSpec((1, seq, LANES), lambda b, h: (b, 0, B_HEADS + h)),
                  pl.BlockSpec((1, seq, LANES), lambda b, h: (b, 0, 2 * B_HEADS + h))],
        out_specs=pl.BlockSpec((1, seq, LANES), lambda b, h: (b, 0, h)),
        out_shape=jax.ShapeDtypeStruct((batch, seq, B_V), BF16),
        scratch_shapes=_attn_scratch(seq),
        compiler_params=_params(("parallel", "parallel")),
        name="diff_attn",
    )(lam_rows, gsub, x, x, x)
    return out.reshape(batch * seq, B_V)


def _latent(qc, kc, vc, batch, seq):
    kern = functools.partial(_attn_kernel, diff=False, seq=seq, lam_init=0.0)
    out = pl.pallas_call(
        kern,
        grid=(batch, C_HEADS // 2),
        in_specs=[pl.BlockSpec((1, seq, 2 * C_PAD), lambda b, h: (b, 0, h)),
                  pl.BlockSpec((1, seq, 2 * C_PAD), lambda b, h: (b, 0, h)),
                  pl.BlockSpec((1, seq, 2 * C_V), lambda b, h: (b, 0, h))],
        out_specs=pl.BlockSpec((1, seq, 2 * C_V), lambda b, h: (b, 0, h)),
        out_shape=jax.ShapeDtypeStruct((batch, seq, C_HEADS * C_V), BF16),
        scratch_shapes=_attn_scratch(seq),
        compiler_params=_params(("parallel", "parallel")),
        name="latent_attn",
    )(qc.reshape(batch, seq, -1), kc.reshape(batch, seq, -1), vc.reshape(batch, seq, -1))
    return out.reshape(batch * seq, C_HEADS * C_V)


TOK_TILE = 512
SLOT_TILE = 512
ROW_SUB = 8
ISSUE_GROUP = 8


def _route(sc, bs):
    gscore = []
    for g in range(N_EXPERT_GROUPS):
        a, b, c, d = bs[4 * g:4 * g + 4]
        hi1, lo1 = jnp.maximum(a, b), jnp.minimum(a, b)
        hi2, lo2 = jnp.maximum(c, d), jnp.minimum(c, d)
        gscore.append(jnp.maximum(hi1, hi2) + jnp.maximum(jnp.minimum(hi1, hi2), jnp.maximum(lo1, lo2)))
    picked = []
    for g in range(N_EXPERT_GROUPS):
        ok = None
        for i in range(N_EXPERT_GROUPS):
            if i == g:
                continue
            c = gscore[g] > gscore[i] if i < g else gscore[g] >= gscore[i]
            ok = c if ok is None else ok & c
        picked.append(ok)
    sel = []
    for g in range(N_EXPERT_GROUPS):
        for j in range(EXPERTS_PER_GROUP):
            vj = bs[4 * g + j]
            rank = jnp.zeros(vj.shape, jnp.int32)
            for i in range(EXPERTS_PER_GROUP):
                if i == j:
                    continue
                vi = bs[4 * g + i]
                ahead = vi >= vj if i < j else vi > vj
                rank = rank + ahead.astype(jnp.int32)
            sel.append(picked[g] & (rank < 2))
    picked_w = [jnp.where(sel[e], sc[e], 0.0) for e in range(N_EXPERTS)]
    total = picked_w[0]
    for e in range(1, N_EXPERTS):
        total = total + picked_w[e]
    return sel, [w / total for w in picked_w]


def _outproj_kernel(oa0, la0, oa1, la1, oa2, la2, ob_ref, oc_ref, gt_ref, x_ref,
                    pa_ref, pb_ref, pc_ref, wo_ref, gffn_ref, wrh_ref, wrl_ref, rb_ref,
                    xn_ref, hrow_ref, dest_ref, wcol_ref, cnt_ref, base_ref, route_ref):
    tm = TOK_TILE
    step = pl.program_id(0)

    @pl.when(step == 0)
    def _():
        base_ref[...] = jnp.zeros(base_ref.shape, F32)

    wide = lambda ref: jnp.concatenate([ref[0], ref[1]], axis=1)
    l0, l1, l2 = wide(la0), wide(la1), wide(la2)
    mx = jnp.maximum(jnp.maximum(l0, l1), l2)
    g0, g1, g2 = jnp.exp(l0 - mx), jnp.exp(l1 - mx), jnp.exp(l2 - mx)
    o_a = (g0 * wide(oa0) + g1 * wide(oa1) + g2 * wide(oa2)) / (g0 + g1 + g2)
    ya = _dot(o_a.astype(BF16), pa_ref[...])
    yb = _dot(ob_ref[...], pb_ref[...])
    yc = _dot(oc_ref[...], pc_ref[...])
    merged = (gt_ref[:, 0:D_MODEL].astype(F32) * ya
              + gt_ref[:, D_MODEL:2 * D_MODEL].astype(F32) * yb
              + gt_ref[:, 2 * D_MODEL:3 * D_MODEL].astype(F32) * yc)
    xn = x_ref[...] + _dot(merged.astype(BF16), wo_ref[...])
    xn_ref[...] = xn
    hn = _rms(xn, gffn_ref[...])
    h_hi = hn.astype(BF16)
    h_lo = (hn - h_hi.astype(F32)).astype(BF16)
    _to_row_tiles(hrow_ref, hn)

    logits = _dot_nt(wrh_ref[...], h_hi) + _dot_nt(wrh_ref[...], h_lo) + _dot_nt(wrl_ref[...], h_hi)
    scores = _sigmoid(logits)
    biased = scores + rb_ref[...]
    sc = [scores[e:e + 1, :] for e in range(N_EXPERTS)]
    bs = [biased[e:e + 1, :] for e in range(N_EXPERTS)]
    sel, w = _route(sc, bs)
    e0 = jnp.full((1, tm), N_EXPERTS, jnp.int32)
    e1 = jnp.full((1, tm), -1, jnp.int32)
    for e in range(N_EXPERTS):
        e0 = jnp.minimum(e0, jnp.where(sel[e], e, N_EXPERTS))
        e1 = jnp.maximum(e1, jnp.where(sel[e], e, -1))
    w0 = jnp.zeros((1, tm), F32)
    w1 = jnp.zeros((1, tm), F32)
    sub = lax.broadcasted_iota(jnp.int32, (N_EXPERTS, 1), 0)
    sel_m = jnp.zeros((N_EXPERTS, tm), F32)
    for e in range(N_EXPERTS):
        w0 = jnp.where(e0 == e, w[e], w0)
        w1 = jnp.where(e1 == e, w[e], w1)
        sel_m = jnp.where((sub == e) & sel[e], 1.0, sel_m)
    upper = lax.broadcasted_iota(jnp.int32, (tm, tm), 0) < lax.broadcasted_iota(jnp.int32, (tm, tm), 1)
    prefix = _dot(sel_m.astype(BF16), upper.astype(BF16)) + base_ref[:, 0:1]
    r0 = jnp.zeros((1, tm), F32)
    r1 = jnp.zeros((1, tm), F32)
    for e in range(N_EXPERTS):
        r0 = jnp.where(e0 == e, prefix[e:e + 1, :], r0)
        r1 = jnp.where(e1 == e, prefix[e:e + 1, :], r1)
    base_ref[...] = base_ref[...] + jnp.sum(sel_m, axis=1, keepdims=True)
    cnt_ref[...] = base_ref[...]
    zrow = jnp.zeros((1, tm), jnp.int32)
    route_ref[step] = jnp.concatenate(
        [e0, e1, r0.astype(jnp.int32), r1.astype(jnp.int32), zrow, zrow, zrow, zrow], axis=0)

    @pl.when(step == pl.num_programs(0) - 1)
    def _():
        counts = base_ref[:, 0:1]
        padded = jnp.ceil(counts * (1.0 / SLOT_TILE)) * SLOT_TILE
        routes = route_ref[...]
        ea, eb = routes[:, 0, :], routes[:, 1, :]
        da, db = routes[:, 2, :], routes[:, 3, :]
        start = jnp.zeros((1, 1), F32)
        for e in range(N_EXPERTS):
            s_e = start.astype(jnp.int32)
            da = da + jnp.where(ea == e, s_e, 0)
            db = db + jnp.where(eb == e, s_e, 0)
            start = start + padded[e:e + 1, :]
        dest_ref[:, 0, :] = jnp.concatenate([da, db], axis=1)

    wcol_ref[...] = jnp.concatenate([w0, w1, jnp.zeros((LANES - 2, tm), F32)], axis=0).T


def _outproj(oas, las, ob, oc, gates, xf, pa, pb, pc, wo, gffn, wr_hi, wr_lo, rbias):
    T = xf.shape[0]
    tm = TOK_TILE
    row = lambda w: pl.BlockSpec((tm, w), lambda i: (i, 0))
    ins, specs = [], []
    for o, l in zip(oas, las):
        ins += [o, l]
        specs += [pl.BlockSpec((2, tm, LANES), lambda i: (0, i, 0))] * 2
    ins += [ob, oc, gates, xf, pa, pb, pc, wo, gffn, wr_hi, wr_lo, rbias]
    specs += [row(B_V), row(C_HEADS * C_V), row(3 * D_MODEL), row(D_MODEL)]
    specs += [_const_spec(a.shape) for a in (pa, pb, pc, wo, gffn, wr_hi, wr_lo, rbias)]
    return pl.pallas_call(
        _outproj_kernel,
        grid=(T // tm,),
        in_specs=specs,
        out_specs=[row(D_MODEL), pl.BlockSpec((tm * ROW_SUB, LANES), lambda i: (i, 0)),
                   pl.BlockSpec((T // tm, 1, 2 * tm), lambda i: (0, 0, 0)),
                   row(LANES), pl.BlockSpec((N_EXPERTS, LANES), lambda i: (0, 0))],
        out_shape=[jax.ShapeDtypeStruct((T, D_MODEL), F32), jax.ShapeDtypeStruct((T * ROW_SUB, LANES), F32),
                   jax.ShapeDtypeStruct((T // tm, 1, 2 * tm), jnp.int32), jax.ShapeDtypeStruct((T, LANES), F32),
                   jax.ShapeDtypeStruct((N_EXPERTS, LANES), F32)],
        scratch_shapes=[pltpu.VMEM((N_EXPERTS, LANES), F32), pltpu.VMEM((T // tm, 8, tm), jnp.int32)],
        compiler_params=_params(("arbitrary",)),
        name="outproj",
    )(*ins)


def _slot_plan(counts, n_tok):
    padded = (counts + SLOT_TILE - 1) // SLOT_TILE * SLOT_TILE
    ends = jnp.cumsum(padded)
    offsets = ends - padded
    max_tiles = (2 * n_tok) // SLOT_TILE + N_EXPERTS
    n_tiles = ends[-1:] // SLOT_TILE
    tile_start = jnp.minimum(jnp.arange(max_tiles, dtype=jnp.int32) * SLOT_TILE, ends[-1] - SLOT_TILE)
    tile_expert = jnp.sum((ends[None, :] <= tile_start[:, None]).astype(jnp.int32), axis=1)
    i32 = lambda a: a.astype(jnp.int32)
    return i32(offsets), i32(padded), i32(tile_expert), i32(n_tiles), max_tiles


def _to_row_tiles(ref, x):
    n = x.shape[0]
    for j in range(ROW_SUB):
        ref[pl.ds(j, n, stride=ROW_SUB), :] = x[:, j * LANES:(j + 1) * LANES]


def _from_row_tiles(ref, n):
    return jnp.concatenate([ref[pl.ds(j, n, stride=ROW_SUB), :] for j in range(ROW_SUB)], axis=1)


def _row_tile(ref, r):
    return ref.at[pl.ds(pl.multiple_of(r * ROW_SUB, ROW_SUB), ROW_SUB)]


def _dispatch_kernel(off_ref, pad_ref, nt_ref, dest_ref, hrow_ref, xs_ref, stage, zero_ref, load_sem, row_sem, zsem):
    tm = TOK_TILE
    i = pl.program_id(0)
    n = pl.num_programs(0)
    tok_rows = tm * ROW_SUB
    tile_rows = SLOT_TILE * ROW_SUB
    max_tiles = xs_ref.shape[0] // tile_rows
    slots = stage.shape[0]

    def zero_tile(t):
        return pltpu.make_async_copy(
            zero_ref, xs_ref.at[pl.ds(pl.multiple_of(t * tile_rows, tile_rows), tile_rows)], zsem)

    def load(t, slot):
        src = hrow_ref.at[pl.ds(pl.multiple_of(t * tok_rows, tok_rows), tok_rows)]
        return pltpu.make_async_copy(src, stage.at[slot], load_sem.at[slot])

    def wait_rows(slot):
        for _ in range(2):
            pltpu.make_async_copy(stage.at[slot], xs_ref.at[pl.ds(0, tok_rows)], row_sem.at[slot]).wait()

    @pl.when(i == 0)
    def _():
        zero_ref[...] = jnp.zeros(zero_ref.shape, F32)

        def fill(t, c):
            zero_tile(t).start()
            return c

        def drain(t, c):
            zero_tile(t).wait()
            return c

        lax.fori_loop(nt_ref[0], max_tiles, fill, 0)
        lax.fori_loop(nt_ref[0], max_tiles, drain, 0)
        for e in range(N_EXPERTS):
            @pl.when(pad_ref[e] > 0)
            def _():
                zero_tile((off_ref[e] + pad_ref[e]) // SLOT_TILE - 1).start()
        for e in range(N_EXPERTS):
            @pl.when(pad_ref[e] > 0)
            def _():
                zero_tile(0).wait()
        load(0, 0).start()

    slot = i % slots
    nxt = (i + 1) % slots

    @pl.when(i >= slots - 1)
    def _():
        wait_rows(nxt)

    @pl.when(i + 1 < n)
    def _():
        load(i + 1, nxt).start()

    load(i, slot).wait()

    def issue(g, c):
        rows = [g * ISSUE_GROUP + u for u in range(ISSUE_GROUP)]
        dsts = [(dest_ref[r], dest_ref[tm + r]) for r in rows]
        for r, (d0, d1) in zip(rows, dsts):
            src = _row_tile(stage.at[slot], r)
            pltpu.make_async_copy(src, _row_tile(xs_ref, d0), row_sem.at[slot]).start(priority=0)
            pltpu.make_async_copy(src, _row_tile(xs_ref, d1), row_sem.at[slot]).start(priority=1)
        return c

    lax.fori_loop(0, tm // ISSUE_GROUP, issue, 0)

    @pl.when(i == n - 1)
    def _():
        for back in range(slots - 2, -1, -1):
            @pl.when(i >= back)
            def _():
                wait_rows((i - back) % slots)


def _dispatch(offsets, padded, n_tiles, dest, hrow, n_slots):
    tm = TOK_TILE
    stage_slots = 3
    gs = pltpu.PrefetchScalarGridSpec(
        num_scalar_prefetch=3,
        grid=(dest.shape[0],),
        in_specs=[pl.BlockSpec((None, None, 2 * tm), lambda i, o, p, n: (i, 0, 0), memory_space=pltpu.SMEM),
                  pl.BlockSpec(memory_space=pl.ANY)],
        out_specs=pl.BlockSpec(memory_space=pl.ANY),
        scratch_shapes=[pltpu.VMEM((stage_slots, tm * ROW_SUB, LANES), F32),
                        pltpu.VMEM((SLOT_TILE * ROW_SUB, LANES), F32),
                        pltpu.SemaphoreType.DMA((stage_slots,)), pltpu.SemaphoreType.DMA((stage_slots,)),
                        pltpu.SemaphoreType.DMA],
    )
    return pl.pallas_call(
        _dispatch_kernel, grid_spec=gs,
        out_shape=jax.ShapeDtypeStruct((n_slots * ROW_SUB, LANES), F32),
        compiler_params=_params(("arbitrary",)),
        name="dispatch",
    )(offsets, padded, n_tiles, dest, hrow)


def _ffn_kernel(te_ref, nt_ref, xs_ref, wg_ref, wu_ref, wd_ref, ys_ref, wg_b, wu_b, wd_b):
    i = pl.program_id(0)
    used = i < nt_ref[0]

    @pl.when(jnp.logical_not(used))
    def _():
        ys_ref[...] = jnp.zeros(ys_ref.shape, F32)

    @pl.when(used & ((i == 0) | (te_ref[i] != te_ref[jnp.maximum(i - 1, 0)])))
    def _():
        wg_b[...] = wg_ref[0].astype(BF16)
        wu_b[...] = wu_ref[0].astype(BF16)
        wd_b[...] = wd_ref[0].astype(BF16)

    @pl.when(used)
    def _():
        x = _from_row_tiles(xs_ref, SLOT_TILE).astype(BF16)
        a = _dot(x, wg_b[...])
        b = _dot(x, wu_b[...])
        act = a * _sigmoid(a) * b
        _to_row_tiles(ys_ref, _dot(act.astype(BF16), wd_b[...]))


def _ffn(tile_expert, n_tiles, xs, wg, wu, wd, layer, max_tiles):
    rows = SLOT_TILE * ROW_SUB
    used = lambda i, te, nt: (jnp.minimum(i, nt[0] - 1), 0)
    wspec = lambda shape: pl.BlockSpec((None, 1) + shape, lambda i, te, nt: (layer, te[i], 0, 0))
    gs = pltpu.PrefetchScalarGridSpec(
        num_scalar_prefetch=2,
        grid=(max_tiles,),
        in_specs=[pl.BlockSpec((rows, LANES), used),
                  wspec((D_MODEL, D_EXPERT)), wspec((D_MODEL, D_EXPERT)), wspec((D_EXPERT, D_MODEL))],
        out_specs=pl.BlockSpec((rows, LANES), lambda i, te, nt: (i, 0)),
        scratch_shapes=[pltpu.VMEM((D_MODEL, D_EXPERT), BF16), pltpu.VMEM((D_MODEL, D_EXPERT), BF16),
                        pltpu.VMEM((D_EXPERT, D_MODEL), BF16)],
    )
    return pl.pallas_call(
        _ffn_kernel, grid_spec=gs,
        out_shape=jax.ShapeDtypeStruct(xs.shape, F32),
        compiler_params=_params(("arbitrary",)),
        name="expert_ffn",
    )(tile_expert, n_tiles, xs, wg, wu, wd)


def _combine_kernel(dest_ref, next_dest_ref, wcol_ref, x_ref, gfin_ref, ys_ref, o_ref, buf, sem, *, final):
    tm = TOK_TILE
    i = pl.program_id(0)
    last = pl.num_programs(0) - 1

    def gather(d_ref, slot):
        def issue(g, c):
            rows = [g * ISSUE_GROUP + u for u in range(ISSUE_GROUP)]
            slots = [(d_ref[r], d_ref[tm + r]) for r in rows]
            for r, (d0, d1) in zip(rows, slots):
                pltpu.make_async_copy(_row_tile(ys_ref, d0), _row_tile(buf.at[slot, 0], r), sem.at[slot]).start(priority=0)
                pltpu.make_async_copy(_row_tile(ys_ref, d1), _row_tile(buf.at[slot, 1], r), sem.at[slot]).start(priority=1)
            return c

        lax.fori_loop(0, tm // ISSUE_GROUP, issue, 0)

    slot = i % 2

    @pl.when(i == 0)
    def _():
        gather(dest_ref, 0)

    @pl.when(i < last)
    def _():
        gather(next_dest_ref, 1 - slot)

    whole = pl.ds(0, tm * ROW_SUB)
    for k in range(2):
        pltpu.make_async_copy(ys_ref.at[whole], buf.at[slot, k], sem.at[slot]).wait()
    w = wcol_ref[...]
    y = (x_ref[...] + w[:, 0:1] * _from_row_tiles(buf.at[slot, 0], tm)
         + w[:, 1:2] * _from_row_tiles(buf.at[slot, 1], tm))
    if final:
        y = _rms(y, gfin_ref[...])
    o_ref[...] = y


def _combine(dest, wcol, xf, gfin, ys, final):
    tm = TOK_TILE
    n = dest.shape[0]
    gs = pl.GridSpec(
        grid=(n,),
        in_specs=[pl.BlockSpec((None, None, 2 * tm), lambda i: (i, 0, 0), memory_space=pltpu.SMEM),
                  pl.BlockSpec((None, None, 2 * tm), lambda i: (jnp.minimum(i + 1, n - 1), 0, 0),
                               memory_space=pltpu.SMEM),
                  pl.BlockSpec((tm, LANES), lambda i: (i, 0)),
                  pl.BlockSpec((tm, D_MODEL), lambda i: (i, 0)),
                  pl.BlockSpec((1, D_MODEL), lambda i: (0, 0)),
                  pl.BlockSpec(memory_space=pl.ANY)],
        out_specs=pl.BlockSpec((tm, D_MODEL), lambda i: (i, 0)),
        scratch_shapes=[pltpu.VMEM((2, 2, tm * ROW_SUB, LANES), F32), pltpu.SemaphoreType.DMA((2,))],
    )
    return pl.pallas_call(
        functools.partial(_combine_kernel, final=final), grid_spec=gs,
        out_shape=jax.ShapeDtypeStruct(xf.shape, F32),
        compiler_params=_params(("arbitrary",)),
        name="combine",
    )(dest, dest, wcol, xf, gfin, ys)


def _alibi_slopes(n):
    return [2.0 ** (-8.0 * h / n) for h in range(1, n + 1)]


def _rope_tables(seq):
    half = C_ROPE // 2
    inv = ROPE_THETA ** (-jnp.arange(half, dtype=F32) * 2.0 / C_ROPE)
    ang = jnp.arange(seq, dtype=F32)[:, None] * inv[None, :]
    cos, sin = jnp.cos(ang), jnp.sin(ang)
    ones = jnp.ones((seq, C_NOPE), F32)
    zeros_n = jnp.zeros((seq, C_NOPE), F32)
    tail = jnp.zeros((seq, C_PAD - C_NOPE - C_ROPE), F32)
    ta = jnp.concatenate([ones, cos, cos, tail], axis=1)
    tb = jnp.concatenate([zeros_n, -sin, sin, tail], axis=1)
    return ta, tb


def _pack_in_tail(w):
    kr = w[:, 4224:4256]
    gate = w[:, 4256:]
    half = C_ROPE // 2
    z = lambda n: jnp.zeros((w.shape[0], n), w.dtype)
    kr_main = jnp.concatenate([z(C_NOPE), kr, z(C_PAD - C_NOPE - C_ROPE)], axis=1)
    kr_swap = jnp.concatenate([z(C_NOPE), kr[:, half:], kr[:, :half], z(C_PAD - C_NOPE - C_ROPE)], axis=1)
    return jnp.concatenate([kr_main, kr_swap, gate], axis=1)


def _pack_latent_weights(w_uq, w_ukv):
    dq = C_NOPE + C_ROPE
    half = C_ROPE // 2
    rows_q = w_uq.shape[0]
    zq = lambda n: jnp.zeros((rows_q, n), w_uq.dtype)
    main, swap = [], []
    for h in range(C_HEADS):
        blk = w_uq[:, h * dq:(h + 1) * dq]
        main += [blk, zq(C_PAD - dq)]
        swap += [zq(C_NOPE), blk[:, C_NOPE + half:], blk[:, C_NOPE:C_NOPE + half], zq(C_PAD - dq)]
    wq2 = jnp.concatenate(main + swap, axis=1).astype(BF16)
    rows_k = w_ukv.shape[0]
    zk = jnp.zeros((rows_k, C_PAD - C_NOPE), w_ukv.dtype)
    dkv = C_NOPE + C_V
    kparts, vparts = [], []
    for h in range(C_HEADS):
        blk = w_ukv[:, h * dkv:(h + 1) * dkv]
        kparts += [blk[:, :C_NOPE], zk]
        vparts.append(blk[:, C_NOPE:])
    wkv2 = jnp.concatenate(kparts + vparts, axis=1).astype(BF16)
    return wq2, wkv2


def kernel(x, w_in, g_mix, p_a, p_b, p_c, w_o, g_q, w_uq, g_kv, w_ukv, lam_q1, lam_k1, lam_q2, lam_k2,
           g_sub, g_ffn, w_router, router_bias, w_gate, w_up, w_down, g_final):
    batch, seq, _ = x.shape
    xf = x.reshape(batch * seq, D_MODEL)
    ta, tb = _rope_tables(seq)
    slopes_a = _alibi_slopes(A_HEADS)
    wr = w_router.T
    wr_hi = wr.astype(BF16)
    wr_lo = (wr - wr_hi.astype(F32)).astype(BF16)
    rbias = router_bias.reshape(N_EXPERTS, 1)
    gfin = g_final.reshape(1, D_MODEL)
    w_in_b = w_in.astype(BF16)

    for l in range(DEPTH):
        w_tail = _pack_in_tail(w_in_b[l])
        wq2, wkv2 = _pack_latent_weights(w_uq[l], w_ukv[l])
        qkva, qkvb, qc, kc, vc, gates = _inproj(
            xf, g_mix[l].reshape(1, -1), w_in_b, l, w_tail, g_q[l].reshape(1, -1), g_kv[l].reshape(1, -1),
            wq2, wkv2, ta, tb, seq)

        oas, las = [], []
        for g, (_, dilation) in enumerate(DIL_PAIRS):
            o, lse = _dilated(qkva, g, dilation, slopes_a[g * A_SLOTS:(g + 1) * A_SLOTS], batch, seq)
            oas.append(o)
            las.append(lse)

        lam_init = 0.8 - 0.6 * math.exp(-0.3 * l)
        lam_rows = jnp.stack([lam_q1[l], lam_k1[l], lam_q2[l], lam_k2[l]])
        ob = _diff(qkvb, lam_rows, g_sub[l].reshape(1, -1), lam_init, batch, seq)
        oc = _latent(qc, kc, vc, batch, seq)

        xf, hrow, dest, wcol, cnt = _outproj(oas, las, ob, oc, gates, xf,
                                            p_a[l].astype(BF16), p_b[l].astype(BF16), p_c[l].astype(BF16),
                                            w_o[l].astype(BF16), g_ffn[l].reshape(1, -1), wr_hi, wr_lo, rbias)
        offsets, padded, tile_expert, n_tiles, max_tiles = _slot_plan(cnt[:, 0].astype(jnp.int32), batch * seq)
        xs = _dispatch(offsets, padded, n_tiles, dest, hrow, max_tiles * SLOT_TILE)
        ys = _ffn(tile_expert, n_tiles, xs, w_gate, w_up, w_down, l, max_tiles)
        xf = _combine(dest, wcol, xf, gfin, ys, final=(l == DEPTH - 1))
    return xf.reshape(batch, seq, D_MODEL)
```
